```python
import jax, jax.numpy as jnp
from jax import lax
import numpy as np

D_MODEL = 2048
BATCH = 4
SEQ = 2048
DEPTH = 2
DEC_BATCH = 128
DEC_SEQ = 1
PAST_LEN = 16384
PAGE_SIZE = 128

D_MIX = D_MODEL
D_HGRN = D_MIX // 2
HGRN_DK = 128
HGRN_DV = 128
HGRN_HEADS = D_HGRN // HGRN_DV
D_FORGET = HGRN_HEADS * HGRN_DK
D_POOL = D_MIX - D_HGRN
POOL_WINDOWS = (2, 4, 8, 16)
N_POOL = len(POOL_WINDOWS)
D_PG = D_POOL // N_POOL
POOL_BUF = max(POOL_WINDOWS) - 1
D_IN = 2 * D_FORGET + 2 * D_HGRN + D_POOL
D_FF = -(-(8 * D_MODEL) // (3 * 256)) * 256
PLE_DIM = 256
CHUNK = 16
EPS = 1e-6

kernel_name = "hymba_hgrn2_multiscale_pool_decoder"


def _rmsnorm(x, g):
    xf = x.astype(jnp.float32)
    y = xf * lax.rsqrt(jnp.mean(xf * xf, axis=-1, keepdims=True) + EPS)
    return (y * g.astype(jnp.float32)).astype(x.dtype)


def _hgrn2_chunked(q, k, v, log_f, S0):
    B, T, H, _ = q.shape
    C = min(CHUNK, T)
    n = -(-T // C)
    pad = n * C - T
    if pad:
        pz = ((0, 0), (0, pad), (0, 0), (0, 0))
        q, k, v, log_f = (jnp.pad(a, pz) for a in (q, k, v, log_f))

    def to_chunks(a):
        return a.reshape(B, n, C, H, a.shape[-1]).transpose(1, 0, 3, 2, 4)

    qc, kc, vc, gc = (to_chunks(a) for a in (q, k, v, log_f))
    mask = jnp.tril(jnp.ones((C, C), dtype=bool))[:, :, None]

    def step(S, inp):
        qb, kb, vb, gb = inp
        b = jnp.cumsum(gb, axis=-2)
        diff = b[..., :, None, :] - b[..., None, :, :]
        decay = jnp.exp(jnp.where(mask, diff, -jnp.inf))
        A = jnp.einsum('bhtd,bhsd,bhtsd->bhts', qb, kb, decay)
        o = (jnp.einsum('bhts,bhsv->bhtv', A, vb)
             + jnp.einsum('bhtd,bhdv->bhtv', qb * jnp.exp(b), S))
        bl = b[..., -1:, :]
        S_new = (jnp.exp(bl[..., 0, :])[..., None] * S
                 + jnp.einsum('bhsd,bhsv->bhdv', kb * jnp.exp(bl - b), vb))
        return S_new, o

    S, o = lax.scan(step, S0, (qc, kc, vc, gc))
    o = o.transpose(1, 0, 3, 2, 4).reshape(B, n * C, H, -1)[:, :T]
    return o, S


def _multiscale_pool(u, buf, start_pos):
    B, T, _ = u.shape
    ext = jnp.concatenate([buf.astype(u.dtype), u], axis=1)
    cs = jnp.cumsum(ext.astype(jnp.float32), axis=1)
    cs = jnp.pad(cs, ((0, 0), (1, 0), (0, 0)))
    pos = start_pos + jnp.arange(T)
    outs = []
    for gi, w in enumerate(POOL_WINDOWS):
        lo, hi = gi * D_PG, (gi + 1) * D_PG
        s = cs[:, POOL_BUF + 1:POOL_BUF + 1 + T, lo:hi] - cs[:, POOL_BUF + 1 - w:POOL_BUF + 1 - w + T, lo:hi]
        cnt = jnp.minimum(pos + 1, w).astype(jnp.float32)
        outs.append(s / cnt[None, :, None])
    mean = jnp.concatenate(outs, axis=-1)
    return mean - u.astype(jnp.float32), ext[:, -POOL_BUF:]


def _layer(h, p_l, S0, buf, start_pos, lb, g_mix, w_in, g_head, w_pool, pool_scale, w_out,
           g_ffn, w_gate_up, w_down, w_ple, g_ple, w_ple_gate):
    B, T, _ = h.shape
    n1 = _rmsnorm(h, g_mix)
    proj = n1 @ w_in
    o1, o2, o3, o4 = D_FORGET, 2 * D_FORGET, 2 * D_FORGET + D_HGRN, 2 * D_FORGET + 2 * D_HGRN
    q_raw, f_raw, v, g_raw, u = proj[..., :o1], proj[..., o1:o2], proj[..., o2:o3], proj[..., o3:o4], proj[..., o4:]

    q = jax.nn.silu(q_raw.astype(jnp.float32)) * (HGRN_DK ** -0.5)
    fg = lb + (1.0 - lb) * jax.nn.sigmoid(f_raw.astype(jnp.float32))
    k = 1.0 - fg
    log_f = jnp.log(fg)
    def heads(a, d):
        return a.reshape(B, T, HGRN_HEADS, d)
    o, S_new = _hgrn2_chunked(heads(q, HGRN_DK), heads(k, HGRN_DK), heads(v.astype(jnp.float32), HGRN_DV),
                              heads(log_f, HGRN_DK), S0.astype(jnp.float32))
    o = _rmsnorm(o, g_head).reshape(B, T, D_HGRN) * jax.nn.silu(g_raw.astype(jnp.float32))

    pooled, buf_new = _multiscale_pool(u, buf, start_pos)
    pm = jnp.einsum('btgc,gcd->btgd', pooled.reshape(B, T, N_POOL, D_PG), w_pool.astype(jnp.float32))
    pm = pm.reshape(B, T, D_POOL) * pool_scale.astype(jnp.float32)

    mix = jnp.concatenate([o, pm], axis=-1).astype(h.dtype)
    h = h + mix @ w_out

    gu = _rmsnorm(h, g_ffn) @ w_gate_up
    h = h + (jax.nn.silu(gu[..., :D_FF]) * gu[..., D_FF:]) @ w_down

    e = _rmsnorm(p_l @ w_ple, g_ple)
    h = h + jax.nn.sigmoid(h @ w_ple_gate) * e
    return h, S_new, buf_new


def _trunk(h, p, S_all, buf_all, start_pos, lbs, g_mix, w_in, g_head, w_pool, pool_scale, w_out,
           g_ffn, w_gate_up, w_down, w_ple, g_ple, w_ple_gate, g_final):
    S_out, buf_out = [], []
    for i in range(DEPTH):
        h, S_i, b_i = _layer(h, p[i], S_all[i], buf_all[i], start_pos, lbs[i], g_mix[i], w_in[i], g_head[i],
                             w_pool[i], pool_scale[i], w_out[i], g_ffn[i], w_gate_up[i], w_down[i],
                             w_ple[i], g_ple[i], w_ple_gate[i])
        S_out.append(S_i)
        buf_out.append(b_i)
    return _rmsnorm(h, g_final), jnp.stack(S_out), jnp.stack(buf_out)


def setup_inputs(seed: int = 0) -> dict:
    key = jax.random.key(seed)
    ks = jax.random.split(key, 20)
    f32 = jnp.float32
    def nrm(k, shape, scale):
        return jax.random.normal(k, shape, f32) * scale
    def gain(k, shape):
        return 1.0 + 0.02 * jax.random.normal(k, shape, f32)
    return {
        'x_prompt': nrm(ks[0], (BATCH, SEQ, D_MODEL), 1.0),
        'x_sample': nrm(ks[1], (DEC_BATCH, DEC_SEQ, D_MODEL), 1.0),
        'state_hgrn': nrm(ks[2], (DEPTH, DEC_BATCH, HGRN_HEADS, HGRN_DK, HGRN_DV), 0.5),
        'state_pool': nrm(ks[3], (DEPTH, DEC_BATCH, POOL_BUF, D_POOL), 1.0),
        'p_prompt': nrm(ks[4], (DEPTH, BATCH, SEQ, PLE_DIM), 1.0),
        'p_sample': nrm(ks[5], (DEPTH, DEC_BATCH, DEC_SEQ, PLE_DIM), 1.0),
        'g_mix': gain(ks[6], (DEPTH, D_MODEL)),
        'w_in': nrm(ks[7], (DEPTH, D_MODEL, D_IN), D_MODEL ** -0.5),
        'lb_logits': nrm(ks[8], (DEPTH, D_FORGET), 0.5),
        'g_head': gain(ks[9], (DEPTH, HGRN_DV)),
        'w_pool': nrm(ks[10], (DEPTH, N_POOL, D_PG, D_PG), D_PG ** -0.5),
        'pool_scale': gain(ks[11], (DEPTH, D_POOL)),
        'w_out': nrm(ks[12], (DEPTH, D_MIX, D_MODEL), D_MIX ** -0.5),
        'g_ffn': gain(ks[13], (DEPTH, D_MODEL)),
        'w_gate_up': nrm(ks[14], (DEPTH, D_MODEL, 2 * D_FF), D_MODEL ** -0.5),
        'w_down': nrm(ks[15], (DEPTH, D_FF, D_MODEL), D_FF ** -0.5),
        'w_ple': nrm(ks[16], (DEPTH, PLE_DIM, D_MODEL), PLE_DIM ** -0.5),
        'g_ple': gain(ks[17], (DEPTH, D_MODEL)),
        'w_ple_gate': nrm(ks[18], (DEPTH, D_MODEL, D_MODEL), D_MODEL ** -0.5),
        'g_final': gain(ks[19], (D_MODEL,)),
    }


def reference(x_prompt, x_sample, state_hgrn, state_pool, p_prompt, p_sample, g_mix, w_in, lb_logits,
              g_head, w_pool, pool_scale, w_out, g_ffn, w_gate_up, w_down, w_ple, g_ple, w_ple_gate, g_final):
    lbs = jnp.cumsum(jax.nn.softmax(lb_logits.astype(jnp.float32), axis=0), axis=0)
    lbs = lbs - lbs[:1]
    weights = (g_mix, w_in, g_head, w_pool, pool_scale, w_out, g_ffn, w_gate_up, w_down,
               w_ple, g_ple, w_ple_gate, g_final)
    S0 = jnp.zeros((DEPTH, BATCH, HGRN_HEADS, HGRN_DK, HGRN_DV), jnp.float32)
    b0 = jnp.zeros((DEPTH, BATCH, POOL_BUF, D_POOL), x_prompt.dtype)
    y_prompt, new_hgrn_prompt, new_pool_prompt = _trunk(x_prompt, p_prompt, S0, b0, 0, lbs, *weights)
    y_sample, new_hgrn_sample, new_pool_sample = _trunk(x_sample, p_sample, state_hgrn, state_pool,
                                                        PAST_LEN, lbs, *weights)
    return (y_prompt, y_sample, new_hgrn_prompt, new_pool_prompt, new_hgrn_sample, new_pool_sample)
```

```python
import functools

import numpy as np
import jax
import jax.numpy as jnp
from jax import lax
from jax.experimental import pallas as pl
from jax.experimental.pallas import tpu as pltpu

F32 = jnp.float32
BF16 = jnp.bfloat16

EPS = 1e-6
HGRN_DK = 128
HGRN_DV = 128
POOL_WINDOWS = (2, 4, 8, 16)
PAST_LEN = 16384
CHUNK = 128
LEVELS = (64, 32, 16, 8, 4, 2, 1)

V7X_VMEM_BYTES = 64 * 1024 * 1024
VMEM_LIMIT_BYTES = V7X_VMEM_BYTES - 8 * 1024 * 1024


def _cparams(n_axes):
    return pltpu.CompilerParams(
        dimension_semantics=("arbitrary",) * n_axes, vmem_limit_bytes=VMEM_LIMIT_BYTES)


def _silu(x):
    return x * jax.nn.sigmoid(x)


def _rms(x, g):
    return x * lax.rsqrt(jnp.mean(x * x, axis=-1, keepdims=True) + EPS) * g


def _rmsnorm_kernel(x_ref, g_ref, o_ref):
    o_ref[...] = _rms(x_ref[...], g_ref[...]).astype(o_ref.dtype)


def _rmsnorm(x, g, out_dtype, bm):
    m, d = x.shape
    return pl.pallas_call(
        _rmsnorm_kernel,
        grid=(m // bm,),
        in_specs=[pl.BlockSpec((bm, d), lambda i: (i, 0)),
                  pl.BlockSpec((1, d), lambda i: (0, 0))],
        out_specs=pl.BlockSpec((bm, d), lambda i: (i, 0)),
        out_shape=jax.ShapeDtypeStruct((m, d), out_dtype),
        compiler_params=_cparams(1),
        name="rmsnorm",
    )(x, g.reshape(1, d))


def _mm_kernel(*refs, n_x, n_w, mode):
    x_refs = refs[:n_x]
    w_refs = refs[n_x:n_x + n_w]
    n_extra = {"plain": 0, "res": 1, "gate": 2, "norm": 1, "swiglu": 0}[mode]
    extra = refs[n_x + n_w:n_x + n_w + n_extra]
    o_ref = refs[n_x + n_w + n_extra]
    wbf = refs[n_x + n_w + n_extra + 1:]

    @pl.when(pl.program_id(1) == 0)
    def _():
        for w_ref, wb in zip(w_refs, wbf):
            wb[...] = w_ref[...].astype(BF16)

    xs = [xr[...].astype(BF16) for xr in x_refs]
    xb = xs[0] if n_x == 1 else jnp.concatenate(xs, axis=1)
    acc = jnp.dot(xb, wbf[0][...], preferred_element_type=F32)
    if mode == "plain":
        out = acc
    elif mode == "res":
        out = extra[0][...] + acc
    elif mode == "gate":
        out = extra[0][...] + jax.nn.sigmoid(acc) * extra[1][...]
    elif mode == "norm":
        out = _rms(acc, extra[0][...])
    else:
        up = jnp.dot(xb, wbf[1][...], preferred_element_type=F32)
        out = _silu(acc) * up
    o_ref[...] = out.astype(o_ref.dtype)


def _matmul(xs, w, layer, *, n_out, bm, bn, mode="plain", extras=(), out_dtype=F32,
            up_offset=None, name="mm"):
    m = xs[0].shape[0]
    k = w.shape[1]
    assert sum(x.shape[1] for x in xs) == k and m % bm == 0 and n_out % bn == 0
    grid = (n_out // bn, m // bm)
    in_specs = [pl.BlockSpec((bm, x.shape[1]), lambda j, i: (i, 0)) for x in xs]
    operands = list(xs)
    in_specs.append(pl.BlockSpec((None, k, bn), lambda j, i: (layer, 0, j)))
    operands.append(w)
    n_w = 1
    if mode == "swiglu":
        off = up_offset // bn
        in_specs.append(pl.BlockSpec((None, k, bn), lambda j, i: (layer, 0, j + off)))
        operands.append(w)
        n_w = 2
    for e in extras:
        if e.shape[0] == 1:
            in_specs.append(pl.BlockSpec((1, bn), lambda j, i: (0, j)))
        else:
            in_specs.append(pl.BlockSpec((bm, bn), lambda j, i: (i, j)))
        operands.append(e)
    return pl.pallas_call(
        functools.partial(_mm_kernel, n_x=len(xs), n_w=n_w, mode=mode),
        grid=grid,
        in_specs=in_specs,
        out_specs=pl.BlockSpec((bm, bn), lambda j, i: (i, j)),
        out_shape=jax.ShapeDtypeStruct((m, n_out), out_dtype),
        scratch_shapes=[pltpu.VMEM((k, bn), BF16) for _ in range(n_w)],
        compiler_params=_cparams(2),
        name=name,
    )(*operands)


def _level_matrix():
    c = CHUNK
    mats = [np.tril(np.ones((c, c), np.float32))]
    for h in LEVELS:
        mt = np.zeros((c, c), np.float32)
        for t in range(c):
            s0 = (t // (2 * h)) * 2 * h
            r = s0 + h - 1
            if t >= s0 + h:
                mt[t, r + 1:t + 1] = 1.0
            else:
                mt[t, t + 1:r + 1] = 1.0
        mats.append(mt)
    return jnp.asarray(np.concatenate(mats, axis=0), dtype=BF16)


def _lower_bound(lb_ref, layer):
    lg = lb_ref[...]
    e = jnp.exp(lg - jnp.max(lg, axis=0, keepdims=True))
    s = e / jnp.sum(e, axis=0, keepdims=True)
    c = s[0:1, :]
    for r in range(1, layer + 1):
        c = c + s[r:r + 1, :]
    return c - s[0:1, :]


def _gates(q_raw, f_raw, lb):
    q = _silu(q_raw) * (HGRN_DK ** -0.5)
    fg = lb + (1.0 - lb) * jax.nn.sigmoid(f_raw)
    return q, 1.0 - fg, fg


def _split2(x):
    hi = x.astype(BF16)
    lo = (x - hi.astype(F32)).astype(BF16)
    return jnp.concatenate([hi, lo], axis=1)


def _hgrn_chunk(q_raw, f_raw, v, lb, lvl_ref, st):
    c = CHUNK
    q, k, fg = _gates(q_raw, f_raw, lb)
    g = jnp.log(fg)
    z2 = jnp.dot(lvl_ref[...], _split2(g), preferred_element_type=F32)
    z = z2[:, :HGRN_DK] + z2[:, HGRN_DK:]
    b = z[0:c, :]
    row = lax.broadcasted_iota(jnp.int32, (c, c), 0)
    col = lax.broadcasted_iota(jnp.int32, (c, c), 1)
    a = jnp.zeros((c, c), F32)
    for li, h in enumerate(LEVELS):
        e = jnp.exp(z[(li + 1) * c:(li + 2) * c, :])
        upper = (row & (2 * h - 1)) >= h
        qh = jnp.where(upper, q * e, 0.0).astype(BF16)
        kh = jnp.where(upper, 0.0, k * e).astype(BF16)
        ah = lax.dot_general(qh, kh, (((1,), (1,)), ((), ())), preferred_element_type=F32)
        sh = h.bit_length()
        a = a + jnp.where((row >> sh) == (col >> sh), ah, 0.0)
    vb = v.astype(BF16)
    o = jnp.dot(a.astype(BF16), vb, preferred_element_type=F32)
    o = o + jnp.sum(q * k, axis=1, keepdims=True) * v
    qb = (q * jnp.exp(b)).astype(BF16)
    o = o + lax.dot_general(qb, st.astype(BF16), (((1,), (1,)), ((), ())),
                            preferred_element_type=F32)
    bl = b[c - 1:c, :]
    kd = (k * jnp.exp(bl - b)).astype(BF16)
    st_new = st * jnp.exp(bl) + jnp.dot(v.T.astype(BF16), kd, preferred_element_type=F32)
    return o, st_new


def _hgrn_prompt_kernel(q_ref, f_ref, v_ref, g_ref, lb_ref, gh_ref, lvl_ref, o_ref, s_ref, st_ref,
                        *, layer, n_chunks):
    t = pl.program_id(2)

    @pl.when(t == 0)
    def _():
        st_ref[...] = jnp.zeros_like(st_ref)

    lb = _lower_bound(lb_ref, layer)
    gh = gh_ref[layer:layer + 1, :]

    def body(ci, carry):
        rows = pl.ds(pl.multiple_of(ci * CHUNK, CHUNK), CHUNK)
        o, st_new = _hgrn_chunk(q_ref[rows, :], f_ref[rows, :], v_ref[rows, :], lb, lvl_ref,
                                st_ref[...])
        st_ref[...] = st_new
        o_ref[rows, :] = (_rms(o, gh) * _silu(g_ref[rows, :])).astype(o_ref.dtype)
        return carry

    lax.fori_loop(0, n_chunks, body, 0)

    @pl.when(t == pl.num_programs(2) - 1)
    def _():
        s_ref[...] = st_ref[...].T


def _hgrn_prompt(proj, lb_logits, g_head, lvl, layer, batch, seq, heads, tc):
    nt = seq // tc
    col = lambda grp: (lambda b, h, t: (b * nt + t, grp * heads + h))
    depth = lb_logits.shape[0]
    return pl.pallas_call(
        functools.partial(_hgrn_prompt_kernel, layer=layer, n_chunks=tc // CHUNK),
        grid=(batch, heads, nt),
        in_specs=[pl.BlockSpec((tc, HGRN_DK), col(0)),
                  pl.BlockSpec((tc, HGRN_DK), col(1)),
                  pl.BlockSpec((tc, HGRN_DV), col(2)),
                  pl.BlockSpec((tc, HGRN_DV), col(3)),
                  pl.BlockSpec((depth, HGRN_DK), lambda b, h, t: (0, h)),
                  pl.BlockSpec((depth, HGRN_DV), lambda b, h, t: (0, 0)),
                  pl.BlockSpec(lvl.shape, lambda b, h, t: (0, 0))],
        out_specs=[pl.BlockSpec((tc, HGRN_DV), lambda b, h, t: (b * nt + t, h)),
                   pl.BlockSpec((None, None, HGRN_DK, HGRN_DV), lambda b, h, t: (b, h, 0, 0))],
        out_shape=[jax.ShapeDtypeStruct((batch * seq, heads * HGRN_DV), BF16),
                   jax.ShapeDtypeStruct((batch, heads, HGRN_DK, HGRN_DV), F32)],
        scratch_shapes=[pltpu.VMEM((HGRN_DV, HGRN_DK), F32)],
        compiler_params=_cparams(3),
        name="hgrn_prompt",
    )(proj, proj, proj, proj, lb_logits, g_head, lvl)


def _hgrn_sample_kernel(q_ref, f_ref, v_ref, g_ref, lb_ref, gh_ref, s_ref, o_ref, sn_ref, orow_ref,
                        *, layer, dec_batch):
    lb = _lower_bound(lb_ref, layer)
    q, k, fg = _gates(q_ref[...], f_ref[...], lb)
    qt, kt, ft = q.T, k.T, fg.T
    for bi in range(dec_batch):
        s_new = ft[:, bi:bi + 1] * s_ref[bi] + kt[:, bi:bi + 1] * v_ref[bi:bi + 1, :]
        sn_ref[bi] = s_new
        orow_ref[bi:bi + 1, :] = jnp.sum(qt[:, bi:bi + 1] * s_new, axis=0, keepdims=True)
    gh = gh_ref[layer:layer + 1, :]
    o_ref[...] = (_rms(orow_ref[...], gh) * _silu(g_ref[...])).astype(o_ref.dtype)


def _hgrn_sample(proj, state, lb_logits, g_head, layer, row_blk, dec_batch, heads):
    col = lambda grp: (lambda h: (row_blk, grp * heads + h))
    depth = lb_logits.shape[0]
    return pl.pallas_call(
        functools.partial(_hgrn_sample_kernel, layer=layer, dec_batch=dec_batch),
        grid=(heads,),
        in_specs=[pl.BlockSpec((dec_batch, HGRN_DK), col(0)),
                  pl.BlockSpec((dec_batch, HGRN_DK), col(1)),
                  pl.BlockSpec((dec_batch, HGRN_DV), col(2)),
                  pl.BlockSpec((dec_batch, HGRN_DV), col(3)),
                  pl.BlockSpec((depth, HGRN_DK), lambda h: (0, h)),
                  pl.BlockSpec((depth, HGRN_DV), lambda h: (0, 0)),
                  pl.BlockSpec((None, dec_batch, None, HGRN_DK, HGRN_DV),
                               lambda h: (layer, 0, h, 0, 0))],
        out_specs=[pl.BlockSpec((dec_batch, HGRN_DV), lambda h: (0, h)),
                   pl.BlockSpec((dec_batch, None, HGRN_DK, HGRN_DV), lambda h: (0, h, 0, 0))],
        out_shape=[jax.ShapeDtypeStruct((dec_batch, heads * HGRN_DV), BF16),
                   jax.ShapeDtypeStruct((dec_batch, heads, HGRN_DK, HGRN_DV), F32)],
        scratch_shapes=[pltpu.VMEM((dec_batch, HGRN_DV), F32)],
        compiler_params=_cparams(1),
        name="hgrn_sample",
    )(proj, proj, proj, proj, lb_logits, g_head, state)


POOL_CARRY = 16


def _pool_mix(pooled, wp_ref, ps_ref, o_ref, d_pg):
    for gi in range(len(POOL_WINDOWS)):
        cs = slice(gi * d_pg, (gi + 1) * d_pg)
        pm = jnp.dot(pooled[gi].astype(BF16), wp_ref[gi].astype(BF16), preferred_element_type=F32)
        o_ref[:, cs] = (pm * ps_ref[:, cs]).astype(o_ref.dtype)


def _pool_prompt_kernel(u_ref, wp_ref, ps_ref, o_ref, ext_ref, *, tt, d_pg):
    t = pl.program_id(1)

    @pl.when(t == 0)
    def _():
        ext_ref[0:POOL_CARRY, :] = jnp.zeros((POOL_CARRY, ext_ref.shape[1]), F32)

    @pl.when(t > 0)
    def _():
        ext_ref[0:POOL_CARRY, :] = ext_ref[tt:tt + POOL_CARRY, :]

    ext_ref[POOL_CARRY:POOL_CARRY + tt, :] = u_ref[...]
    pos = t * tt + lax.broadcasted_iota(jnp.int32, (tt, 1), 0)
    pooled = []
    for gi, w in enumerate(POOL_WINDOWS):
        cs = slice(gi * d_pg, (gi + 1) * d_pg)
        s = ext_ref[POOL_CARRY:POOL_CARRY + tt, cs]
        for j in range(1, w):
            s = s + ext_ref[POOL_CARRY - j:POOL_CARRY - j + tt, cs]
        cnt = jnp.minimum(pos + 1, w).astype(F32)
        pooled.append(s / cnt - u_ref[:, cs])
    _pool_mix(pooled, wp_ref, ps_ref, o_ref, d_pg)


def _pool_prompt(proj, w_pool, pool_scale, layer, batch, seq, u_col_blk, tt):
    d_pool = pool_scale.shape[1]
    n_pool, d_pg = w_pool.shape[1], w_pool.shape[2]
    nt = seq // tt
    return pl.pallas_call(
        functools.partial(_pool_prompt_kernel, tt=tt, d_pg=d_pg),
        grid=(batch, nt),
        in_specs=[pl.BlockSpec((tt, d_pool), lambda b, t: (b * nt + t, u_col_blk)),
                  pl.BlockSpec((None, n_pool, d_pg, d_pg), lambda b, t: (layer, 0, 0, 0)),
                  pl.BlockSpec((None, 1, d_pool), lambda b, t: (layer, 0, 0))],
        out_specs=pl.BlockSpec((tt, d_pool), lambda b, t: (b * nt + t, 0)),
        out_shape=jax.ShapeDtypeStruct((batch * seq, d_pool), BF16),
        scratch_shapes=[pltpu.VMEM((tt + POOL_CARRY, d_pool), F32)],
        compiler_params=_cparams(2),
        name="pool_prompt",
    )(proj, w_pool, pool_scale.reshape(-1, 1, d_pool))


def _pool_sample_kernel(u_ref, buf_ref, wp_ref, ps_ref, o_ref, *, d_pg, n_buf):
    pooled = []
    for gi, w in enumerate(POOL_WINDOWS):
        cs = slice(gi * d_pg, (gi + 1) * d_pg)
        u = u_ref[:, cs]
        s = u
        for j in range(1, w):
            s = s + buf_ref[n_buf - j, :, cs]
        cnt = float(min(PAST_LEN + 1, w))
        pooled.append(s / cnt - u)
    _pool_mix(pooled, wp_ref, ps_ref, o_ref, d_pg)


def _pool_sample(proj, buf_t, w_pool, pool_scale, layer, row_blk, u_col_blk, dec_batch):
    d_pool = pool_scale.shape[1]
    n_pool, d_pg = w_pool.shape[1], w_pool.shape[2]
    n_buf = buf_t.shape[0]
    return pl.pallas_call(
        functools.partial(_pool_sample_kernel, d_pg=d_pg, n_buf=n_buf),
        grid=(1,),
        in_specs=[pl.BlockSpec((dec_batch, d_pool), lambda i: (row_blk, u_col_blk)),
                  pl.BlockSpec(buf_t.shape, lambda i: (0, 0, 0)),
                  pl.BlockSpec((None, n_pool, d_pg, d_pg), lambda i: (layer, 0, 0, 0)),
                  pl.BlockSpec((None, 1, d_pool), lambda i: (layer, 0, 0))],
        out_specs=pl.BlockSpec((dec_batch, d_pool), lambda i: (0, 0)),
        out_shape=jax.ShapeDtypeStruct((dec_batch, d_pool), BF16),
        compiler_params=_cparams(1),
        name="pool_sample",
    )(proj, buf_t, w_pool, pool_scale.reshape(-1, 1, d_pool))


def kernel(x_prompt, x_sample, state_hgrn, state_pool, p_prompt, p_sample, g_mix, w_in, lb_logits,
           g_head, w_pool, pool_scale, w_out, g_ffn, w_gate_up, w_down, w_ple, g_ple, w_ple_gate,
           g_final):
    batch, seq, d_model = x_prompt.shape
    dec_batch = x_sample.shape[0]
    depth = w_in.shape[0]
    heads = lb_logits.shape[1] // HGRN_DK
    d_hgrn = heads * HGRN_DV
    d_pool = pool_scale.shape[1]
    d_ff = w_down.shape[1]
    pool_buf = state_pool.shape[2]
    m_prompt = batch * seq
    m = m_prompt + dec_batch
    assert x_sample.shape[1] == 1 and m_prompt % dec_batch == 0
    assert w_in.shape[2] == 2 * heads * HGRN_DK + 2 * d_hgrn + d_pool

    bm = 832
    assert m % bm == 0
    sample_blk = m_prompt // dec_batch
    u_col_blk = (2 * heads * HGRN_DK + 2 * d_hgrn) // d_pool
    lvl = _level_matrix()

    h = jnp.concatenate([x_prompt.reshape(m_prompt, d_model), x_sample.reshape(dec_batch, d_model)], 0)
    p_all = jnp.concatenate([p_prompt.reshape(depth, m_prompt, -1),
                             p_sample.reshape(depth, dec_batch, -1)], axis=1)

    s_prompt, s_sample, pool_prompt, pool_sample = [], [], [], []
    for l in range(depth):
        n1 = _rmsnorm(h, g_mix[l], BF16, bm)
        proj = _matmul([n1], w_in, l, n_out=w_in.shape[2], bm=bm, bn=1024, name="proj_in")

        o_p, s_p = _hgrn_prompt(proj, lb_logits, g_head, lvl, l, batch, seq, heads, tc=512)
        o_s, s_s = _hgrn_sample(proj, state_hgrn, lb_logits, g_head, l, sample_blk, dec_batch, heads)
        pm_p = _pool_prompt(proj, w_pool, pool_scale, l, batch, seq, u_col_blk, tt=512)
        buf_t = jnp.transpose(state_pool[l], (1, 0, 2))
        pm_s = _pool_sample(proj, buf_t, w_pool, pool_scale, l, sample_blk, u_col_blk, dec_batch)
        s_prompt.append(s_p)
        s_sample.append(s_s)
        u = proj[:, u_col_blk * d_pool:]
        pool_prompt.append(u[:m_prompt].reshape(batch, seq, d_pool)[:, seq - pool_buf:])
        pool_sample.append(jnp.concatenate([state_pool[l][:, 1:], u[m_prompt:, None, :]], axis=1))

        o_all = jnp.concatenate([o_p, o_s], axis=0)
        pm_all = jnp.concatenate([pm_p, pm_s], axis=0)
        h = _matmul([o_all, pm_all], w_out, l, n_out=d_model, bm=bm, bn=1024, mode="res",
                    extras=(h,), name="proj_out")

        n2 = _rmsnorm(h, g_ffn[l], BF16, bm)
        act = _matmul([n2], w_gate_up, l, n_out=d_ff, bm=bm, bn=512, mode="swiglu",
                      up_offset=d_ff, out_dtype=BF16, name="ffn_up")
        h = _matmul([act], w_down, l, n_out=d_model, bm=bm // 2, bn=512, mode="res",
                    extras=(h,), name="ffn_down")

        e = _matmul([p_all[l]], w_ple, l, n_out=d_model, bm=bm, bn=d_model, mode="norm",
                    extras=(g_ple[l].reshape(1, d_model),), name="ple_embed")
        h = _matmul([h], w_ple_gate, l, n_out=d_model, bm=bm // 2, bn=1024, mode="gate",
                    extras=(h, e), name="ple_gate")

    y = _rmsnorm(h, g_final, F32, bm)
    y_prompt = y[:m_prompt].reshape(batch, seq, d_model)
    y_sample = y[m_prompt:].reshape(dec_batch, 1, d_model)
    return (y_prompt, y_sample, jnp.stack(s_prompt), jnp.stack(pool_prompt),
            jnp.stack(s_sample), jnp.stack(pool_sample))
```

```python
import functools

import numpy as np
import jax
import jax.numpy as jnp
from jax import lax
from jax.experimental import pallas as pl
from jax.experimental.pallas import tpu as pltpu

F32 = jnp.float32
BF16 = jnp.bfloat16

EPS = 1e-6
HGRN_DK = 128
HGRN_DV = 128
POOL_WINDOWS = (2, 4, 8, 16)
PAST_LEN = 16384
SUBLANES = 8
CHUNK = 128
LEVELS = (64, 32, 16, 8, 4, 2, 1)
HEADS_PER_STEP = 4

V7X_VMEM_BYTES = 64 * 1024 * 1024
VMEM_LIMIT_BYTES = V7X_VMEM_BYTES - 8 * 1024 * 1024


def _cparams(n_axes):
    return pltpu.CompilerParams(
        dimension_semantics=("arbitrary",) * n_axes, vmem_limit_bytes=VMEM_LIMIT_BYTES)


def _silu(x):
    return x * jax.nn.sigmoid(x)


def _rms(x, g):
    return x * lax.rsqrt(jnp.mean(x * x, axis=-1, keepdims=True) + EPS) * g


def _rmsnorm_kernel(xp_ref, xs_ref, g_ref, op_ref, os_ref, *, n_prompt_tiles):
    i = pl.program_id(0)

    @pl.when(i < n_prompt_tiles)
    def _():
        op_ref[...] = _rms(xp_ref[...], g_ref[...]).astype(op_ref.dtype)

    @pl.when(i == n_prompt_tiles)
    def _():
        os_ref[...] = _rms(xs_ref[...], g_ref[...]).astype(os_ref.dtype)


def _rmsnorm(x_p, x_s, g, out_dtype, bm):
    mp, d = x_p.shape
    ms = x_s.shape[0]
    npt = mp // bm
    prow = lambda i: (jnp.minimum(i, npt - 1), 0)
    return pl.pallas_call(
        functools.partial(_rmsnorm_kernel, n_prompt_tiles=npt),
        grid=(npt + 1,),
        in_specs=[pl.BlockSpec((bm, d), prow),
                  pl.BlockSpec((ms, d), lambda i: (0, 0)),
                  pl.BlockSpec((1, d), lambda i: (0, 0))],
        out_specs=[pl.BlockSpec((bm, d), prow),
                   pl.BlockSpec((ms, d), lambda i: (0, 0))],
        out_shape=[jax.ShapeDtypeStruct((mp, d), out_dtype),
                   jax.ShapeDtypeStruct((ms, d), out_dtype)],
        compiler_params=_cparams(1),
        name="rmsnorm",
    )(x_p, x_s, g.reshape(1, d))


_N_EXTRA = {"plain": 0, "res": 1, "gate": 2, "norm": 0, "swiglu": 0}
_N_VEC = {"plain": 0, "res": 0, "gate": 0, "norm": 1, "swiglu": 0}


def _mm_kernel(*refs, n_x, n_w, n_out, mode, n_prompt_tiles):
    n_e, n_v = _N_EXTRA[mode], _N_VEC[mode]
    pos = 0

    def take(n):
        nonlocal pos
        out = refs[pos:pos + n]
        pos += n
        return out

    xp, xs, w_refs = take(n_x), take(n_x), take(n_w)
    ep, es, vecs = take(n_e), take(n_e), take(n_v)
    op, osm, wbf = take(n_out), take(n_out), take(n_w)
    i = pl.program_id(1)

    @pl.when(i == 0)
    def _():
        for w_ref, wb in zip(w_refs, wbf):
            wb[...] = w_ref[...].astype(BF16)

    def compute(x_refs, extra, outs):
        cast = [xr[...].astype(BF16) for xr in x_refs]
        xb = cast[0] if n_x == 1 else jnp.concatenate(cast, axis=1)
        acc = jnp.dot(xb, wbf[0][...], preferred_element_type=F32)
        if mode == "plain":
            res = acc
        elif mode == "res":
            res = extra[0][...] + acc
        elif mode == "gate":
            res = extra[0][...] + jax.nn.sigmoid(acc) * extra[1][...]
        elif mode == "norm":
            res = _rms(acc, vecs[0][...])
        else:
            res = _silu(acc) * jnp.dot(xb, wbf[1][...], preferred_element_type=F32)
        for o_ref in outs:
            o_ref[...] = res.astype(o_ref.dtype)

    @pl.when(i < n_prompt_tiles)
    def _():
        compute(xp, ep, op)

    @pl.when(i == n_prompt_tiles)
    def _():
        compute(xs, es, osm)


def _matmul(xs_p, xs_s, w, layer, *, n_cols, bm, bn, mode="plain", extras_p=(), extras_s=(), vecs=(),
            out_dtypes=(F32,), up_offset=None, name="mm"):
    mp, ms = xs_p[0].shape[0], xs_s[0].shape[0]
    k = w.shape[1]
    assert sum(x.shape[1] for x in xs_p) == k and mp % bm == 0 and n_cols % bn == 0
    npt = mp // bm
    grid = (n_cols // bn, npt + 1)
    prow = lambda j, i: (jnp.minimum(i, npt - 1), 0)
    ptile = lambda j, i: (jnp.minimum(i, npt - 1), j)
    in_specs = [pl.BlockSpec((bm, x.shape[1]), prow) for x in xs_p]
    in_specs += [pl.BlockSpec((ms, x.shape[1]), lambda j, i: (0, 0)) for x in xs_s]
    in_specs.append(pl.BlockSpec((None, k, bn), lambda j, i: (layer, 0, j)))
    operands = list(xs_p) + list(xs_s) + [w]
    n_w = 1
    if mode == "swiglu":
        off = up_offset // bn
        in_specs.append(pl.BlockSpec((None, k, bn), lambda j, i: (layer, 0, j + off)))
        operands.append(w)
        n_w = 2
    in_specs += [pl.BlockSpec((bm, bn), ptile) for _ in extras_p]
    in_specs += [pl.BlockSpec((ms, bn), lambda j, i: (0, j)) for _ in extras_s]
    in_specs += [pl.BlockSpec((1, bn), lambda j, i: (0, j)) for _ in vecs]
    operands += list(extras_p) + list(extras_s) + list(vecs)
    out_specs = ([pl.BlockSpec((bm, bn), ptile) for _ in out_dtypes]
                 + [pl.BlockSpec((ms, bn), lambda j, i: (0, j)) for _ in out_dtypes])
    out_shape = ([jax.ShapeDtypeStruct((mp, n_cols), dt) for dt in out_dtypes]
                 + [jax.ShapeDtypeStruct((ms, n_cols), dt) for dt in out_dtypes])
    return pl.pallas_call(
        functools.partial(_mm_kernel, n_x=len(xs_p), n_w=n_w, n_out=len(out_dtypes), mode=mode,
                          n_prompt_tiles=npt),
        grid=grid,
        in_specs=in_specs,
        out_specs=out_specs,
        out_shape=out_shape,
        scratch_shapes=[pltpu.VMEM((k, bn), BF16) for _ in range(n_w)],
        compiler_params=_cparams(2),
        name=name,
    )(*operands)


def _chunk_constants():
    c = CHUNK
    t = np.arange(c)[:, None]
    s = np.arange(c)[None, :]
    masks = []
    for h in LEVELS:
        same = (t // (2 * h)) == (s // (2 * h))
        masks.append((same & ((t % (2 * h)) >= h) & ((s % (2 * h)) < h)).astype(np.float32))
    tri = (s <= t).astype(np.float32)
    return jnp.asarray(tri, dtype=BF16), jnp.asarray(np.stack(masks), dtype=F32)


def _lower_bound(lb_ref, layer):
    lg = lb_ref[...]
    e = jnp.exp(lg - jnp.max(lg, axis=0, keepdims=True))
    s = e / jnp.sum(e, axis=0, keepdims=True)
    c = s[0:1, :]
    for r in range(1, layer + 1):
        c = c + s[r:r + 1, :]
    return c - s[0:1, :]


def _gates(q_raw, f_raw, lb):
    q = _silu(q_raw) * (HGRN_DK ** -0.5)
    fg = lb + (1.0 - lb) * jax.nn.sigmoid(f_raw)
    return q, 1.0 - fg, fg


def _split2(x):
    hi = x.astype(BF16)
    lo = (x - hi.astype(F32)).astype(BF16)
    return jnp.concatenate([hi, lo], axis=1)


def _boundary_rows(b, h):
    n_grp = CHUNK // SUBLANES

    def bcast(grp, sub):
        r = grp * SUBLANES + sub
        return jnp.broadcast_to(b[r:r + 1, :], (SUBLANES, b.shape[1]))

    sub_id = lax.broadcasted_iota(jnp.int32, (SUBLANES, b.shape[1]), 0)
    pieces = []
    for grp in range(n_grp):
        if h >= SUBLANES:
            per_blk = 2 * h // SUBLANES
            pieces.append(bcast((grp // per_blk) * per_blk + per_blk // 2 - 1, SUBLANES - 1))
        else:
            piece = bcast(grp, h - 1)
            for blk in range(1, SUBLANES // (2 * h)):
                piece = jnp.where(sub_id < blk * 2 * h, piece, bcast(grp, blk * 2 * h + h - 1))
            pieces.append(piece)
    return jnp.concatenate(pieces, axis=0)


def _hgrn_chunk(q_raw, f_raw, v, lb, tri_ref, mask_ref, st):
    c = CHUNK
    q, k, fg = _gates(q_raw, f_raw, lb)
    b2 = jnp.dot(tri_ref[...], _split2(jnp.log(fg)), preferred_element_type=F32)
    b = b2[:, :HGRN_DK] + b2[:, HGRN_DK:]
    a = jnp.zeros((c, c), F32)
    for li, h in enumerate(LEVELS):
        e = jnp.exp(-jnp.abs(b - _boundary_rows(b, h)))
        ah = lax.dot_general((q * e).astype(BF16), (k * e).astype(BF16), (((1,), (1,)), ((), ())),
                             preferred_element_type=F32)
        a = a + ah * mask_ref[li]
    o = jnp.dot(a.astype(BF16), v.astype(BF16), preferred_element_type=F32)
    o = o + jnp.sum(q * k, axis=1, keepdims=True) * v
    qb = (q * jnp.exp(b)).astype(BF16)
    o = o + lax.dot_general(qb, st.astype(BF16), (((1,), (1,)), ((), ())),
                            preferred_element_type=F32)
    bl = b[c - 1:c, :]
    kd = (k * jnp.exp(bl - b)).astype(BF16)
    st_new = st * jnp.exp(bl) + jnp.dot(v.T.astype(BF16), kd, preferred_element_type=F32)
    return o, st_new


def _hgrn_prompt_kernel(q_ref, f_ref, v_ref, g_ref, lb_ref, gh_ref, tri_ref, mask_ref, o_ref, s_ref,
                        st_ref, *, layer, n_chunks):
    st_ref[...] = jnp.zeros_like(st_ref)
    lb_all = _lower_bound(lb_ref, layer)
    gh = gh_ref[layer:layer + 1, :]

    def body(ci, carry):
        rows = pl.ds(pl.multiple_of(ci * CHUNK, CHUNK), CHUNK)
        for hh in range(HEADS_PER_STEP):
            cs = slice(hh * HGRN_DK, (hh + 1) * HGRN_DK)
            o, st_new = _hgrn_chunk(q_ref[rows, cs], f_ref[rows, cs], v_ref[rows, cs], lb_all[:, cs],
                                    tri_ref, mask_ref, st_ref[hh])
            st_ref[hh] = st_new
            o_ref[rows, cs] = (_rms(o, gh) * _silu(g_ref[rows, cs])).astype(o_ref.dtype)
        return carry

    lax.fori_loop(0, n_chunks, body, 0)
    for hh in range(HEADS_PER_STEP):
        s_ref[hh] = st_ref[hh].T


def _hgrn_prompt(proj, lb_logits, g_head, consts, layer, batch, seq, heads):
    hp = HEADS_PER_STEP
    n_hp = heads // hp
    wid = hp * HGRN_DK
    col = lambda grp: (lambda b, p: (b, grp * n_hp + p))
    depth = lb_logits.shape[0]
    tri, masks = consts
    return pl.pallas_call(
        functools.partial(_hgrn_prompt_kernel, layer=layer, n_chunks=seq // CHUNK),
        grid=(batch, n_hp),
        in_specs=[pl.BlockSpec((seq, wid), col(0)),
                  pl.BlockSpec((seq, wid), col(1)),
                  pl.BlockSpec((seq, wid), col(2)),
                  pl.BlockSpec((seq, wid), col(3)),
                  pl.BlockSpec((depth, wid), lambda b, p: (0, p)),
                  pl.BlockSpec((depth, HGRN_DV), lambda b, p: (0, 0)),
                  pl.BlockSpec(tri.shape, lambda b, p: (0, 0)),
                  pl.BlockSpec(masks.shape, lambda b, p: (0, 0, 0))],
        out_specs=[pl.BlockSpec((seq, wid), lambda b, p: (b, p)),
                   pl.BlockSpec((None, hp, HGRN_DK, HGRN_DV), lambda b, p: (b, p, 0, 0))],
        out_shape=[jax.ShapeDtypeStruct((batch * seq, heads * HGRN_DV), BF16),
                   jax.ShapeDtypeStruct((batch, heads, HGRN_DK, HGRN_DV), F32)],
        scratch_shapes=[pltpu.VMEM((hp, HGRN_DV, HGRN_DK), F32)],
        compiler_params=_cparams(2),
        name="hgrn_prompt",
    )(proj, proj, proj, proj, lb_logits, g_head, tri, masks)


def _hgrn_sample_kernel(q_ref, f_ref, v_ref, g_ref, lb_ref, gh_ref, s_ref, o_ref, sn_ref, orow_ref,
                        *, layer, dec_batch):
    lb = _lower_bound(lb_ref, layer)
    q, k, fg = _gates(q_ref[...], f_ref[...], lb)
    qt, kt, ft = q.T, k.T, fg.T
    for bi in range(dec_batch):
        s_new = ft[:, bi:bi + 1] * s_ref[bi] + kt[:, bi:bi + 1] * v_ref[bi:bi + 1, :]
        sn_ref[bi] = s_new
        orow_ref[bi:bi + 1, :] = jnp.sum(qt[:, bi:bi + 1] * s_new, axis=0, keepdims=True)
    gh = gh_ref[layer:layer + 1, :]
    o_ref[...] = (_rms(orow_ref[...], gh) * _silu(g_ref[...])).astype(o_ref.dtype)


def _hgrn_sample(proj, state, lb_logits, g_head, layer, dec_batch, heads):
    col = lambda grp: (lambda h: (0, grp * heads + h))
    depth = lb_logits.shape[0]
    return pl.pallas_call(
        functools.partial(_hgrn_sample_kernel, layer=layer, dec_batch=dec_batch),
        grid=(heads,),
        in_specs=[pl.BlockSpec((dec_batch, HGRN_DK), col(0)),
                  pl.BlockSpec((dec_batch, HGRN_DK), col(1)),
                  pl.BlockSpec((dec_batch, HGRN_DV), col(2)),
                  pl.BlockSpec((dec_batch, HGRN_DV), col(3)),
                  pl.BlockSpec((depth, HGRN_DK), lambda h: (0, h)),
                  pl.BlockSpec((depth, HGRN_DV), lambda h: (0, 0)),
                  pl.BlockSpec((None, dec_batch, None, HGRN_DK, HGRN_DV),
                               lambda h: (layer, 0, h, 0, 0))],
        out_specs=[pl.BlockSpec((dec_batch, HGRN_DV), lambda h: (0, h)),
                   pl.BlockSpec((dec_batch, None, HGRN_DK, HGRN_DV), lambda h: (0, h, 0, 0))],
        out_shape=[jax.ShapeDtypeStruct((dec_batch, heads * HGRN_DV), BF16),
                   jax.ShapeDtypeStruct((dec_batch, heads, HGRN_DK, HGRN_DV), F32)],
        scratch_shapes=[pltpu.VMEM((dec_batch, HGRN_DV), F32)],
        compiler_params=_cparams(1),
        name="hgrn_sample",
    )(proj, proj, proj, proj, lb_logits, g_head, state)


POOL_CARRY = 16


def _pool_mix(pooled, wp_ref, ps_ref, o_ref, d_pg):
    for gi in range(len(POOL_WINDOWS)):
        cs = slice(gi * d_pg, (gi + 1) * d_pg)
        pm = jnp.dot(pooled[gi].astype(BF16), wp_ref[gi].astype(BF16), preferred_element_type=F32)
        o_ref[:, cs] = (pm * ps_ref[:, cs]).astype(o_ref.dtype)


def _pool_prompt_kernel(u_ref, wp_ref, ps_ref, o_ref, ext_ref, *, tt, d_pg):
    t = pl.program_id(1)

    @pl.when(t == 0)
    def _():
        ext_ref[0:POOL_CARRY, :] = jnp.zeros((POOL_CARRY, ext_ref.shape[1]), F32)

    @pl.when(t > 0)
    def _():
        ext_ref[0:POOL_CARRY, :] = ext_ref[tt:tt + POOL_CARRY, :]

    ext_ref[POOL_CARRY:POOL_CARRY + tt, :] = u_ref[...]
    pos = t * tt + lax.broadcasted_iota(jnp.int32, (tt, 1), 0)
    pooled = []
    for gi, w in enumerate(POOL_WINDOWS):
        cs = slice(gi * d_pg, (gi + 1) * d_pg)
        s = ext_ref[POOL_CARRY:POOL_CARRY + tt, cs]
        for j in range(1, w):
            s = s + ext_ref[POOL_CARRY - j:POOL_CARRY - j + tt, cs]
        cnt = jnp.minimum(pos + 1, w).astype(F32)
        pooled.append(s / cnt - u_ref[:, cs])
    _pool_mix(pooled, wp_ref, ps_ref, o_ref, d_pg)


def _pool_prompt(proj, w_pool, pool_scale, layer, batch, seq, u_col_blk, tt):
    d_pool = pool_scale.shape[1]
    n_pool, d_pg = w_pool.shape[1], w_pool.shape[2]
    nt = seq // tt
    return pl.pallas_call(
        functools.partial(_pool_prompt_kernel, tt=tt, d_pg=d_pg),
        grid=(batch, nt),
        in_specs=[pl.BlockSpec((tt, d_pool), lambda b, t: (b * nt + t, u_col_blk)),
                  pl.BlockSpec((None, n_pool, d_pg, d_pg), lambda b, t: (layer, 0, 0, 0)),
                  pl.BlockSpec((None, 1, d_pool), lambda b, t: (layer, 0, 0))],
        out_specs=pl.BlockSpec((tt, d_pool), lambda b, t: (b * nt + t, 0)),
        out_shape=jax.ShapeDtypeStruct((batch * seq, d_pool), BF16),
        scratch_shapes=[pltpu.VMEM((tt + POOL_CARRY, d_pool), F32)],
        compiler_params=_cparams(2),
        name="pool_prompt",
    )(proj, w_pool, pool_scale.reshape(-1, 1, d_pool))


def _pool_sample_kernel(u_ref, buf_ref, wp_ref, ps_ref, o_ref, *, d_pg, n_buf):
    pooled = []
    for gi, w in enumerate(POOL_WINDOWS):
        cs = slice(gi * d_pg, (gi + 1) * d_pg)
        u = u_ref[:, cs]
        s = u
        for j in range(1, w):
            s = s + buf_ref[n_buf - j, :, cs]
        cnt = float(min(PAST_LEN + 1, w))
        pooled.append(s / cnt - u)
    _pool_mix(pooled, wp_ref, ps_ref, o_ref, d_pg)


def _pool_sample(proj, buf_t, w_pool, pool_scale, layer, u_col_blk, dec_batch):
    d_pool = pool_scale.shape[1]
    n_pool, d_pg = w_pool.shape[1], w_pool.shape[2]
    n_buf = buf_t.shape[0]
    return pl.pallas_call(
        functools.partial(_pool_sample_kernel, d_pg=d_pg, n_buf=n_buf),
        grid=(1,),
        in_specs=[pl.BlockSpec((dec_batch, d_pool), lambda i: (0, u_col_blk)),
                  pl.BlockSpec(buf_t.shape, lambda i: (0, 0, 0)),
                  pl.BlockSpec((None, n_pool, d_pg, d_pg), lambda i: (layer, 0, 0, 0)),
                  pl.BlockSpec((None, 1, d_pool), lambda i: (layer, 0, 0))],
        out_specs=pl.BlockSpec((dec_batch, d_pool), lambda i: (0, 0)),
        out_shape=jax.ShapeDtypeStruct((dec_batch, d_pool), BF16),
        compiler_params=_cparams(1),
        name="pool_sample",
    )(proj, buf_t, w_pool, pool_scale.reshape(-1, 1, d_pool))


def kernel(x_prompt, x_sample, state_hgrn, state_pool, p_prompt, p_sample, g_mix, w_in, lb_logits,
           g_head, w_pool, pool_scale, w_out, g_ffn, w_gate_up, w_down, w_ple, g_ple, w_ple_gate,
           g_final):
    batch, seq, d_model = x_prompt.shape
    dec_batch = x_sample.shape[0]
    depth = w_in.shape[0]
    heads = lb_logits.shape[1] // HGRN_DK
    d_hgrn = heads * HGRN_DV
    d_pool = pool_scale.shape[1]
    d_ff = w_down.shape[1]
    pool_buf = state_pool.shape[2]
    mp = batch * seq
    assert x_sample.shape[1] == 1 and seq % CHUNK == 0 and heads % HEADS_PER_STEP == 0
    assert w_in.shape[2] == 2 * heads * HGRN_DK + 2 * d_hgrn + d_pool

    bm = 1024
    u_col_blk = (2 * heads * HGRN_DK + 2 * d_hgrn) // d_pool
    consts = _chunk_constants()

    h_p = x_prompt.reshape(mp, d_model)
    h_s = x_sample.reshape(dec_batch, d_model)
    pe_p = p_prompt.reshape(depth, mp, -1)
    pe_s = p_sample.reshape(depth, dec_batch, -1)

    s_prompt, s_sample, pool_prompt, pool_sample = [], [], [], []
    for l in range(depth):
        n1_p, n1_s = _rmsnorm(h_p, h_s, g_mix[l], BF16, bm)
        proj_p, proj_s = _matmul([n1_p], [n1_s], w_in, l, n_cols=w_in.shape[2], bm=bm, bn=1024,
                                 name="proj_in")

        o_p, st_p = _hgrn_prompt(proj_p, lb_logits, g_head, consts, l, batch, seq, heads)
        o_s, st_s = _hgrn_sample(proj_s, state_hgrn, lb_logits, g_head, l, dec_batch, heads)
        pm_p = _pool_prompt(proj_p, w_pool, pool_scale, l, batch, seq, u_col_blk, tt=512)
        buf_t = jnp.transpose(state_pool[l], (1, 0, 2))
        pm_s = _pool_sample(proj_s, buf_t, w_pool, pool_scale, l, u_col_blk, dec_batch)
        s_prompt.append(st_p)
        s_sample.append(st_s)
        u_off = u_col_blk * d_pool
        pool_prompt.append(proj_p.reshape(batch, seq, -1)[:, seq - pool_buf:, u_off:])
        pool_sample.append(jnp.concatenate([state_pool[l][:, 1:], proj_s[:, None, u_off:]], axis=1))

        h_p, h_s = _matmul([o_p, pm_p], [o_s, pm_s], w_out, l, n_cols=d_model, bm=bm, bn=1024,
                           mode="res", extras_p=(h_p,), extras_s=(h_s,), name="proj_out")

        n2_p, n2_s = _rmsnorm(h_p, h_s, g_ffn[l], BF16, bm)
        act_p, act_s = _matmul([n2_p], [n2_s], w_gate_up, l, n_cols=d_ff, bm=bm, bn=512,
                               mode="swiglu", up_offset=d_ff, out_dtypes=(BF16,), name="ffn_up")
        h_p, hb_p, h_s, hb_s = _matmul([act_p], [act_s], w_down, l, n_cols=d_model, bm=bm // 2, bn=512,
                                       mode="res", extras_p=(h_p,), extras_s=(h_s,),
                                       out_dtypes=(F32, BF16), name="ffn_down")

        e_p, e_s = _matmul([pe_p[l]], [pe_s[l]], w_ple, l, n_cols=d_model, bm=bm, bn=d_model,
                           mode="norm", vecs=(g_ple[l].reshape(1, d_model),), name="ple_embed")
        h_p, h_s = _matmul([hb_p], [hb_s], w_ple_gate, l, n_cols=d_model, bm=bm // 2, bn=1024,
                           mode="gate", extras_p=(h_p, e_p), extras_s=(h_s, e_s), name="ple_gate")

    y_p, y_s = _rmsnorm(h_p, h_s, g_final, F32, bm)
    return (y_p.reshape(batch, seq, d_model), y_s.reshape(dec_batch, 1, d_model),
            jnp.stack(s_prompt), jnp.stack(pool_prompt), jnp.stack(s_sample), jnp.stack(pool_sample))
```

```python
import functools

import numpy as np
import jax
import jax.numpy as jnp
from jax import lax
from jax.experimental import pallas as pl
from jax.experimental.pallas import tpu as pltpu

F32 = jnp.float32
BF16 = jnp.bfloat16

EPS = 1e-6
LOG2E = 1.4426950408889634
HGRN_DK = 128
HGRN_DV = 128
POOL_WINDOWS = (2, 4, 8, 16)
PAST_LEN = 16384
SUBLANES = 8
CHUNK = 128
LEVELS = (64, 32, 16, 8, 4, 2, 1)
HEADS_PER_STEP = 4

V7X_VMEM_BYTES = 64 * 1024 * 1024
VMEM_LIMIT_BYTES = V7X_VMEM_BYTES - 8 * 1024 * 1024


def _cparams(n_axes):
    return pltpu.CompilerParams(
        dimension_semantics=("arbitrary",) * n_axes, vmem_limit_bytes=VMEM_LIMIT_BYTES)


def _silu(x):
    return x * jax.nn.sigmoid(x)


def _rms(x, g):
    return x * lax.rsqrt(jnp.mean(x * x, axis=-1, keepdims=True) + EPS) * g


def _rmsnorm_kernel(xp_ref, xs_ref, g_ref, op_ref, os_ref, *, n_prompt_tiles):
    i = pl.program_id(0)

    @pl.when(i < n_prompt_tiles)
    def _():
        op_ref[...] = _rms(xp_ref[...], g_ref[...]).astype(op_ref.dtype)

    @pl.when(i == n_prompt_tiles)
    def _():
        os_ref[...] = _rms(xs_ref[...], g_ref[...]).astype(os_ref.dtype)


def _rmsnorm(x_p, x_s, g, out_dtype, bm):
    mp, d = x_p.shape
    ms = x_s.shape[0]
    npt = mp // bm
    prow = lambda i: (jnp.minimum(i, npt - 1), 0)
    return pl.pallas_call(
        functools.partial(_rmsnorm_kernel, n_prompt_tiles=npt),
        grid=(npt + 1,),
        in_specs=[pl.BlockSpec((bm, d), prow),
                  pl.BlockSpec((ms, d), lambda i: (0, 0)),
                  pl.BlockSpec((1, d), lambda i: (0, 0))],
        out_specs=[pl.BlockSpec((bm, d), prow),
                   pl.BlockSpec((ms, d), lambda i: (0, 0))],
        out_shape=[jax.ShapeDtypeStruct((mp, d), out_dtype),
                   jax.ShapeDtypeStruct((ms, d), out_dtype)],
        compiler_params=_cparams(1),
        name="rmsnorm",
    )(x_p, x_s, g.reshape(1, d))


_N_EXTRA = {"plain": 0, "res": 1, "swiglu": 0}


def _mm_kernel(*refs, n_x, n_w, n_out, mode):
    n_e = _N_EXTRA[mode]
    pos = 0

    def take(n):
        nonlocal pos
        out = refs[pos:pos + n]
        pos += n
        return out

    xp, xs, w_refs = take(n_x), take(n_x), take(n_w)
    ep, es = take(n_e), take(n_e)
    op, osm, wbf = take(n_out), take(n_out), take(n_w)
    i = pl.program_id(1)

    @pl.when(i == 0)
    def _():
        for w_ref, wb in zip(w_refs, wbf):
            wb[...] = w_ref[...].astype(BF16)

    def compute(x_refs, extra, outs):
        cast = [xr[...].astype(BF16) for xr in x_refs]
        xb = cast[0] if n_x == 1 else jnp.concatenate(cast, axis=1)
        acc = jnp.dot(xb, wbf[0][...], preferred_element_type=F32)
        if mode == "plain":
            res = acc
        elif mode == "res":
            res = extra[0][...] + acc
        else:
            res = _silu(acc) * jnp.dot(xb, wbf[1][...], preferred_element_type=F32)
        for o_ref in outs:
            o_ref[...] = res.astype(o_ref.dtype)

    @pl.when(i == 0)
    def _():
        compute(xs, es, osm)

    @pl.when(i > 0)
    def _():
        compute(xp, ep, op)


def _matmul(xs_p, xs_s, w, layer, *, n_cols, bm, bn, mode="plain", extras_p=(), extras_s=(),
            out_dtypes=(F32,), up_offset=None, name="mm"):
    mp, ms = xs_p[0].shape[0], xs_s[0].shape[0]
    k = w.shape[1]
    assert sum(x.shape[1] for x in xs_p) == k and mp % bm == 0 and n_cols % bn == 0
    npt = mp // bm
    grid = (n_cols // bn, npt + 1)
    prow = lambda j, i: (jnp.maximum(i - 1, 0), 0)
    ptile = lambda j, i: (jnp.maximum(i - 1, 0), j)
    in_specs = [pl.BlockSpec((bm, x.shape[1]), prow) for x in xs_p]
    in_specs += [pl.BlockSpec((ms, x.shape[1]), lambda j, i: (0, 0)) for x in xs_s]
    in_specs.append(pl.BlockSpec((None, k, bn), lambda j, i: (layer, 0, j)))
    operands = list(xs_p) + list(xs_s) + [w]
    n_w = 1
    if mode == "swiglu":
        off = up_offset // bn
        in_specs.append(pl.BlockSpec((None, k, bn), lambda j, i: (layer, 0, j + off)))
        operands.append(w)
        n_w = 2
    in_specs += [pl.BlockSpec((bm, bn), ptile) for _ in extras_p]
    in_specs += [pl.BlockSpec((ms, bn), lambda j, i: (0, j)) for _ in extras_s]
    operands += list(extras_p) + list(extras_s)
    out_specs = ([pl.BlockSpec((bm, bn), ptile) for _ in out_dtypes]
                 + [pl.BlockSpec((ms, bn), lambda j, i: (0, j)) for _ in out_dtypes])
    out_shape = ([jax.ShapeDtypeStruct((mp, n_cols), dt) for dt in out_dtypes]
                 + [jax.ShapeDtypeStruct((ms, n_cols), dt) for dt in out_dtypes])
    return pl.pallas_call(
        functools.partial(_mm_kernel, n_x=len(xs_p), n_w=n_w, n_out=len(out_dtypes), mode=mode),
        grid=grid,
        in_specs=in_specs,
        out_specs=out_specs,
        out_shape=out_shape,
        scratch_shapes=[pltpu.VMEM((k, bn), BF16) for _ in range(n_w)],
        compiler_params=_cparams(2),
        name=name,
    )(*operands)


def _resident(shape, index_map):
    return pl.BlockSpec(shape, index_map, pipeline_mode=pl.Buffered(1))


def _proj_out_kernel(op_ref, pmp_ref, hp_ref, os_ref, pms_ref, hs_ref, w_ref, g_ref,
                     hp_out, np_out, hs_out, ns_out, wbf):
    i = pl.program_id(0)

    def compute(o_ref, pm_ref, h_ref, h_out, n_out):
        xb = jnp.concatenate([o_ref[...], pm_ref[...]], axis=1)
        hn = h_ref[...] + jnp.dot(xb, wbf[...], preferred_element_type=F32)
        h_out[...] = hn
        n_out[...] = _rms(hn, g_ref[...]).astype(n_out.dtype)

    @pl.when(i == 0)
    def _():
        wbf[...] = w_ref[...].astype(BF16)
        compute(os_ref, pms_ref, hs_ref, hs_out, ns_out)

    @pl.when(i > 0)
    def _():
        compute(op_ref, pmp_ref, hp_ref, hp_out, np_out)


def _proj_out(o_p, pm_p, h_p, o_s, pm_s, h_s, w_out, g_ffn, layer, bm):
    mp, d = h_p.shape
    ms = h_s.shape[0]
    ko, kp = o_p.shape[1], pm_p.shape[1]
    prow = lambda i: (jnp.maximum(i - 1, 0), 0)
    srow = lambda i: (0, 0)
    return pl.pallas_call(
        _proj_out_kernel,
        grid=(mp // bm + 1,),
        in_specs=[pl.BlockSpec((bm, ko), prow), pl.BlockSpec((bm, kp), prow), pl.BlockSpec((bm, d), prow),
                  pl.BlockSpec((ms, ko), srow), pl.BlockSpec((ms, kp), srow), pl.BlockSpec((ms, d), srow),
                  _resident((None, ko + kp, d), lambda i: (layer, 0, 0)),
                  pl.BlockSpec((None, 1, d), lambda i: (layer, 0, 0))],
        out_specs=[pl.BlockSpec((bm, d), prow), pl.BlockSpec((bm, d), prow),
                   pl.BlockSpec((ms, d), srow), pl.BlockSpec((ms, d), srow)],
        out_shape=[jax.ShapeDtypeStruct((mp, d), F32), jax.ShapeDtypeStruct((mp, d), BF16),
                   jax.ShapeDtypeStruct((ms, d), F32), jax.ShapeDtypeStruct((ms, d), BF16)],
        scratch_shapes=[pltpu.VMEM((ko + kp, d), BF16)],
        compiler_params=_cparams(1),
        name="proj_out",
    )(o_p, pm_p, h_p, o_s, pm_s, h_s, w_out, g_ffn.reshape(-1, 1, d))


def _ple_kernel(*refs, write_h):
    hp_ref, pp_ref, hs_ref, ps_ref, wg_ref, wp_ref, ge_ref, gn_ref = refs[:8]
    outs = refs[8:-2]
    wgb, wpb = refs[-2:]
    if write_h:
        hp_out, np_out, hs_out, ns_out = outs
    else:
        (np_out, ns_out), hp_out, hs_out = outs, None, None
    i = pl.program_id(0)

    def compute(h_ref, p_ref, h_out, n_out):
        hv = h_ref[...]
        gate = jax.nn.sigmoid(jnp.dot(hv.astype(BF16), wgb[...], preferred_element_type=F32))
        e = _rms(jnp.dot(p_ref[...].astype(BF16), wpb[...], preferred_element_type=F32), ge_ref[...])
        hn = hv + gate * e
        if write_h:
            h_out[...] = hn
        n_out[...] = _rms(hn, gn_ref[...]).astype(n_out.dtype)

    @pl.when(i == 0)
    def _():
        wgb[...] = wg_ref[...].astype(BF16)
        wpb[...] = wp_ref[...].astype(BF16)
        compute(hs_ref, ps_ref, hs_out, ns_out)

    @pl.when(i > 0)
    def _():
        compute(hp_ref, pp_ref, hp_out, np_out)


def _ple(h_p, pe_p, h_s, pe_s, w_gate, w_ple, g_ple, g_next, layer, bm, *, write_h, norm_dtype):
    mp, d = h_p.shape
    ms = h_s.shape[0]
    kp = pe_p.shape[2]
    prow = lambda i: (jnp.maximum(i - 1, 0), 0)
    srow = lambda i: (0, 0)
    n_specs = [pl.BlockSpec((bm, d), prow), pl.BlockSpec((ms, d), srow)]
    n_shapes = [jax.ShapeDtypeStruct((mp, d), norm_dtype), jax.ShapeDtypeStruct((ms, d), norm_dtype)]
    if write_h:
        out_specs = [n_specs[0], n_specs[0], n_specs[1], n_specs[1]]
        out_shape = [jax.ShapeDtypeStruct((mp, d), F32), n_shapes[0],
                     jax.ShapeDtypeStruct((ms, d), F32), n_shapes[1]]
    else:
        out_specs, out_shape = n_specs, n_shapes
    return pl.pallas_call(
        functools.partial(_ple_kernel, write_h=write_h),
        grid=(mp // bm + 1,),
        in_specs=[pl.BlockSpec((bm, d), prow),
                  pl.BlockSpec((None, bm, kp), lambda i: (layer, jnp.maximum(i - 1, 0), 0)),
                  pl.BlockSpec((ms, d), srow),
                  pl.BlockSpec((None, ms, kp), lambda i: (layer, 0, 0)),
                  _resident((None, d, d), lambda i: (layer, 0, 0)),
                  _resident((None, kp, d), lambda i: (layer, 0, 0)),
                  pl.BlockSpec((None, 1, d), lambda i: (layer, 0, 0)),
                  pl.BlockSpec((1, d), lambda i: (0, 0))],
        out_specs=out_specs,
        out_shape=out_shape,
        scratch_shapes=[pltpu.VMEM((d, d), BF16), pltpu.VMEM((kp, d), BF16)],
        compiler_params=_cparams(1),
        name="ple",
    )(h_p, pe_p, h_s, pe_s, w_gate, w_ple, g_ple.reshape(-1, 1, d), g_next)


def _chunk_constants():
    c = CHUNK
    t = np.arange(c)[:, None]
    s = np.arange(c)[None, :]
    masks, signs = [], []
    for h in LEVELS:
        same = (t // (2 * h)) == (s // (2 * h))
        upper = (t % (2 * h)) >= h
        masks.append((same & upper & ((s % (2 * h)) < h)).astype(np.float32))
        signs.append(np.broadcast_to(np.where(upper, LOG2E, -LOG2E),
                                     (c, HEADS_PER_STEP * HGRN_DK)).astype(np.float32))
    tri = (s <= t).astype(np.float32)
    return (jnp.asarray(tri, dtype=BF16), jnp.asarray(np.stack(masks), dtype=BF16),
            jnp.asarray(np.stack(signs), dtype=F32))


def _lower_bound(lb_ref, layer):
    lg = lb_ref[...]
    e = jnp.exp(lg - jnp.max(lg, axis=0, keepdims=True))
    s = e / jnp.sum(e, axis=0, keepdims=True)
    c = s[0:1, :]
    for r in range(1, layer + 1):
        c = c + s[r:r + 1, :]
    return c - s[0:1, :]


def _gates(q_raw, f_raw, lb):
    q = _silu(q_raw) * (HGRN_DK ** -0.5)
    fg = lb + (1.0 - lb) * jax.nn.sigmoid(f_raw)
    return q, 1.0 - fg, fg


def _split2(x):
    hi = x.astype(BF16)
    lo = (x - hi.astype(F32)).astype(BF16)
    return jnp.concatenate([hi, lo], axis=1)


def _boundary_rows(b, h):
    n_grp = CHUNK // SUBLANES

    def bcast(grp, sub):
        r = grp * SUBLANES + sub
        return jnp.broadcast_to(b[r:r + 1, :], (SUBLANES, b.shape[1]))

    sub_id = lax.broadcasted_iota(jnp.int32, (SUBLANES, b.shape[1]), 0)
    pieces = []
    for grp in range(n_grp):
        if h >= SUBLANES:
            per_blk = 2 * h // SUBLANES
            pieces.append(bcast((grp // per_blk) * per_blk + per_blk // 2 - 1, SUBLANES - 1))
        else:
            piece = bcast(grp, h - 1)
            for blk in range(1, SUBLANES // (2 * h)):
                piece = jnp.where(sub_id < blk * 2 * h, piece, bcast(grp, blk * 2 * h + h - 1))
            pieces.append(piece)
    return jnp.concatenate(pieces, axis=0)


_NT = (((1,), (1,)), ((), ()))


def _hgrn_chunk(q_raw, f_raw, v, lb, tri_ref, mask_ref, sign_ref, st_ref):
    c, w = CHUNK, q_raw.shape[1]
    heads = [slice(i * HGRN_DK, (i + 1) * HGRN_DK) for i in range(w // HGRN_DK)]
    q, k, fg = _gates(q_raw, f_raw, lb)
    b2 = jnp.dot(tri_ref[...], _split2(jnp.log(fg)), preferred_element_type=F32)
    b = b2[:, :w] + b2[:, w:]
    q16, k16, v16 = q.astype(BF16), k.astype(BF16), v.astype(BF16)
    a = [jnp.zeros((c, c), BF16) for _ in heads]
    for li, h in enumerate(LEVELS):
        e = jnp.exp2((b - _boundary_rows(b, h)) * sign_ref[li]).astype(BF16)
        qe, ke = q16 * e, k16 * e
        for i, cs in enumerate(heads):
            ah = lax.dot_general(qe[:, cs], ke[:, cs], _NT, preferred_element_type=F32)
            a[i] = a[i] + ah.astype(BF16) * mask_ref[li]
    qk = q * k
    qb = (q * jnp.exp(b)).astype(BF16)
    bl = b[c - 1:c, :]
    kd = (k * jnp.exp(bl - b)).astype(BF16)
    el = jnp.exp(bl)
    outs = []
    for i, cs in enumerate(heads):
        st = st_ref[i]
        o = jnp.dot(a[i], v16[:, cs], preferred_element_type=F32)
        o = o + jnp.sum(qk[:, cs], axis=1, keepdims=True) * v[:, cs]
        o = o + lax.dot_general(qb[:, cs], st.astype(BF16), _NT, preferred_element_type=F32)
        st_ref[i] = st * el[:, cs] + jnp.dot(v[:, cs].T.astype(BF16), kd[:, cs],
                                             preferred_element_type=F32)
        outs.append(o)
    return outs


def _hgrn_prompt_kernel(q_ref, f_ref, v_ref, g_ref, lb_ref, gh_ref, tri_ref, mask_ref, sign_ref,
                        o_ref, s_ref, st_ref, *, layer, n_chunks):
    st_ref[...] = jnp.zeros_like(st_ref)
    lb = _lower_bound(lb_ref, layer)
    gh = gh_ref[layer:layer + 1, :]

    def body(ci, carry):
        rows = pl.ds(pl.multiple_of(ci * CHUNK, CHUNK), CHUNK)
        outs = _hgrn_chunk(q_ref[rows, :], f_ref[rows, :], v_ref[rows, :], lb, tri_ref, mask_ref,
                           sign_ref, st_ref)
        o = jnp.concatenate([_rms(oh, gh) for oh in outs], axis=1)
        o_ref[rows, :] = (o * _silu(g_ref[rows, :])).astype(o_ref.dtype)
        return carry

    lax.fori_loop(0, n_chunks, body, 0)
    for hh in range(HEADS_PER_STEP):
        s_ref[hh] = st_ref[hh].T


def _hgrn_prompt(proj, lb_logits, g_head, consts, layer, batch, seq, heads):
    hp = HEADS_PER_STEP
    n_hp = heads // hp
    wid = hp * HGRN_DK
    col = lambda grp: (lambda b, p: (b, grp * n_hp + p))
    depth = lb_logits.shape[0]
    tri, masks, signs = consts
    return pl.pallas_call(
        functools.partial(_hgrn_prompt_kernel, layer=layer, n_chunks=seq // CHUNK),
        grid=(batch, n_hp),
        in_specs=[pl.BlockSpec((seq, wid), col(0)),
                  pl.BlockSpec((seq, wid), col(1)),
                  pl.BlockSpec((seq, wid), col(2)),
                  pl.BlockSpec((seq, wid), col(3)),
                  pl.BlockSpec((depth, wid), lambda b, p: (0, p)),
                  pl.BlockSpec((depth, HGRN_DV), lambda b, p: (0, 0)),
                  pl.BlockSpec(tri.shape, lambda b, p: (0, 0)),
                  pl.BlockSpec(masks.shape, lambda b, p: (0, 0, 0)),
                  pl.BlockSpec(signs.shape, lambda b, p: (0, 0, 0))],
        out_specs=[pl.BlockSpec((seq, wid), lambda b, p: (b, p)),
                   pl.BlockSpec((None, hp, HGRN_DK, HGRN_DV), lambda b, p: (b, p, 0, 0))],
        out_shape=[jax.ShapeDtypeStruct((batch * seq, heads * HGRN_DV), BF16),
                   jax.ShapeDtypeStruct((batch, heads, HGRN_DK, HGRN_DV), F32)],
        scratch_shapes=[pltpu.VMEM((hp, HGRN_DV, HGRN_DK), F32)],
        compiler_params=_cparams(2),
        name="hgrn_prompt",
    )(proj, proj, proj, proj, lb_logits, g_head, tri, masks, signs)


def _hgrn_sample_kernel(q_ref, f_ref, v_ref, g_ref, lb_ref, gh_ref, s_ref, *rest, layer, dec_batch):
    o_ref, sn_ref, orow_ref = rest[-3:]
    lb = _lower_bound(lb_ref, layer)
    q, k, fg = _gates(q_ref[...], f_ref[...], lb)
    qt, kt, ft = q.T, k.T, fg.T
    for bi in range(dec_batch):
        s_new = ft[:, bi:bi + 1] * s_ref[bi] + kt[:, bi:bi + 1] * v_ref[bi:bi + 1, :]
        sn_ref[bi] = s_new
        orow_ref[bi:bi + 1, :] = jnp.sum(qt[:, bi:bi + 1] * s_new, axis=0, keepdims=True)
    gh = gh_ref[layer:layer + 1, :]
    o_ref[...] = (_rms(orow_ref[...], gh) * _silu(g_ref[...])).astype(o_ref.dtype)


def _hgrn_sample(proj, state, lb_logits, g_head, layer, dec_batch, heads, new_state=None):
    col = lambda grp: (lambda h: (0, grp * heads + h))
    depth = lb_logits.shape[0]
    slab = pl.BlockSpec((None, dec_batch, None, HGRN_DK, HGRN_DV), lambda h: (layer, 0, h, 0, 0))
    in_specs = [pl.BlockSpec((dec_batch, HGRN_DK), col(0)),
                pl.BlockSpec((dec_batch, HGRN_DK), col(1)),
                pl.BlockSpec((dec_batch, HGRN_DV), col(2)),
                pl.BlockSpec((dec_batch, HGRN_DV), col(3)),
                pl.BlockSpec((depth, HGRN_DK), lambda h: (0, h)),
                pl.BlockSpec((depth, HGRN_DV), lambda h: (0, 0)),
                slab]
    operands = [proj, proj, proj, proj, lb_logits, g_head, state]
    aliases = {}
    if new_state is not None:
        in_specs.append(pl.BlockSpec(memory_space=pl.ANY))
        operands.append(new_state)
        aliases = {len(operands) - 1: 1}
    return pl.pallas_call(
        functools.partial(_hgrn_sample_kernel, layer=layer, dec_batch=dec_batch),
        grid=(heads,),
        in_specs=in_specs,
        out_specs=[pl.BlockSpec((dec_batch, HGRN_DV), lambda h: (0, h)), slab],
        out_shape=[jax.ShapeDtypeStruct((dec_batch, heads * HGRN_DV), BF16),
                   jax.ShapeDtypeStruct(state.shape, F32)],
        scratch_shapes=[pltpu.VMEM((dec_batch, HGRN_DV), F32)],
        input_output_aliases=aliases,
        compiler_params=_cparams(1),
        name="hgrn_sample",
    )(*operands)


POOL_CARRY = 16


def _pool_mix(pooled, wp_ref, ps_ref, o_ref, d_pg):
    for gi in range(len(POOL_WINDOWS)):
        cs = slice(gi * d_pg, (gi + 1) * d_pg)
        pm = jnp.dot(pooled[gi].astype(BF16), wp_ref[gi].astype(BF16), preferred_element_type=F32)
        o_ref[:, cs] = (pm * ps_ref[:, cs]).astype(o_ref.dtype)


def _pool_prompt_kernel(u_ref, wp_ref, ps_ref, o_ref, ext_ref, *, tt, d_pg):
    t = pl.program_id(1)

    @pl.when(t == 0)
    def _():
        ext_ref[0:POOL_CARRY, :] = jnp.zeros((POOL_CARRY, ext_ref.shape[1]), F32)

    @pl.when(t > 0)
    def _():
        ext_ref[0:POOL_CARRY, :] = ext_ref[tt:tt + POOL_CARRY, :]

    ext_ref[POOL_CARRY:POOL_CARRY + tt, :] = u_ref[...]
    pos = t * tt + lax.broadcasted_iota(jnp.int32, (tt, 1), 0)
    pooled = []
    for gi, w in enumerate(POOL_WINDOWS):
        cs = slice(gi * d_pg, (gi + 1) * d_pg)
        s = ext_ref[POOL_CARRY:POOL_CARRY + tt, cs]
        for j in range(1, w):
            s = s + ext_ref[POOL_CARRY - j:POOL_CARRY - j + tt, cs]
        cnt = jnp.minimum(pos + 1, w).astype(F32)
        pooled.append(s / cnt - u_ref[:, cs])
    _pool_mix(pooled, wp_ref, ps_ref, o_ref, d_pg)


def _pool_prompt(proj, w_pool, pool_scale, layer, batch, seq, u_col_blk, tt):
    d_pool = pool_scale.shape[1]
    n_pool, d_pg = w_pool.shape[1], w_pool.shape[2]
    nt = seq // tt
    return pl.pallas_call(
        functools.partial(_pool_prompt_kernel, tt=tt, d_pg=d_pg),
        grid=(batch, nt),
        in_specs=[pl.BlockSpec((tt, d_pool), lambda b, t: (b * nt + t, u_col_blk)),
                  pl.BlockSpec((None, n_pool, d_pg, d_pg), lambda b, t: (layer, 0, 0, 0)),
                  pl.BlockSpec((None, 1, d_pool), lambda b, t: (layer, 0, 0))],
        out_specs=pl.BlockSpec((tt, d_pool), lambda b, t: (b * nt + t, 0)),
        out_shape=jax.ShapeDtypeStruct((batch * seq, d_pool), BF16),
        scratch_shapes=[pltpu.VMEM((tt + POOL_CARRY, d_pool), F32)],
        compiler_params=_cparams(2),
        name="pool_prompt",
    )(proj, w_pool, pool_scale.reshape(-1, 1, d_pool))


def _pool_sample_kernel(u_ref, buf_ref, wp_ref, ps_ref, o_ref, *, d_pg, n_buf):
    pooled = []
    for gi, w in enumerate(POOL_WINDOWS):
        cs = slice(gi * d_pg, (gi + 1) * d_pg)
        u = u_ref[:, cs]
        s = u
        for j in range(1, w):
            s = s + buf_ref[n_buf - j, :, cs]
        cnt = float(min(PAST_LEN + 1, w))
        pooled.append(s / cnt - u)
    _pool_mix(pooled, wp_ref, ps_ref, o_ref, d_pg)


def _pool_sample(proj, buf_t, w_pool, pool_scale, layer, u_col_blk, dec_batch):
    d_pool = pool_scale.shape[1]
    n_pool, d_pg = w_pool.shape[1], w_pool.shape[2]
    n_buf = buf_t.shape[0]
    return pl.pallas_call(
        functools.partial(_pool_sample_kernel, d_pg=d_pg, n_buf=n_buf),
        grid=(1,),
        in_specs=[pl.BlockSpec((dec_batch, d_pool), lambda i: (0, u_col_blk)),
                  pl.BlockSpec(buf_t.shape, lambda i: (0, 0, 0)),
                  pl.BlockSpec((None, n_pool, d_pg, d_pg), lambda i: (layer, 0, 0, 0)),
                  pl.BlockSpec((None, 1, d_pool), lambda i: (layer, 0, 0))],
        out_specs=pl.BlockSpec((dec_batch, d_pool), lambda i: (0, 0)),
        out_shape=jax.ShapeDtypeStruct((dec_batch, d_pool), BF16),
        compiler_params=_cparams(1),
        name="pool_sample",
    )(proj, buf_t, w_pool, pool_scale.reshape(-1, 1, d_pool))


def kernel(x_prompt, x_sample, state_hgrn, state_pool, p_prompt, p_sample, g_mix, w_in, lb_logits,
           g_head, w_pool, pool_scale, w_out, g_ffn, w_gate_up, w_down, w_ple, g_ple, w_ple_gate,
           g_final):
    batch, seq, d_model = x_prompt.shape
    dec_batch = x_sample.shape[0]
    depth = w_in.shape[0]
    heads = lb_logits.shape[1] // HGRN_DK
    d_hgrn = heads * HGRN_DV
    d_pool = pool_scale.shape[1]
    d_ff = w_down.shape[1]
    pool_buf = state_pool.shape[2]
    mp = batch * seq
    assert x_sample.shape[1] == 1 and seq % CHUNK == 0 and heads % HEADS_PER_STEP == 0
    assert w_in.shape[2] == 2 * heads * HGRN_DK + 2 * d_hgrn + d_pool

    bm = 1024
    bm_row = 256
    u_col_blk = (2 * heads * HGRN_DK + 2 * d_hgrn) // d_pool
    consts = _chunk_constants()

    h_p = x_prompt.reshape(mp, d_model)
    h_s = x_sample.reshape(dec_batch, d_model)
    pe_p = p_prompt.reshape(depth, mp, -1)
    pe_s = p_sample.reshape(depth, dec_batch, -1)

    s_prompt, pool_prompt, pool_sample = [], [], []
    new_state = None
    n_p, n_s = _rmsnorm(h_p, h_s, g_mix[0], BF16, bm)
    for l in range(depth):
        proj_p, proj_s = _matmul([n_p], [n_s], w_in, l, n_cols=w_in.shape[2], bm=bm, bn=1024,
                                 name="proj_in")

        o_p, st_p = _hgrn_prompt(proj_p, lb_logits, g_head, consts, l, batch, seq, heads)
        o_s, new_state = _hgrn_sample(proj_s, state_hgrn, lb_logits, g_head, l, dec_batch, heads,
                                      new_state)
        pm_p = _pool_prompt(proj_p, w_pool, pool_scale, l, batch, seq, u_col_blk, tt=512)
        buf_t = jnp.transpose(state_pool[l], (1, 0, 2))
        pm_s = _pool_sample(proj_s, buf_t, w_pool, pool_scale, l, u_col_blk, dec_batch)
        s_prompt.append(st_p)
        u_off = u_col_blk * d_pool
        pool_prompt.append(proj_p.reshape(batch, seq, -1)[:, seq - pool_buf:, u_off:])
        pool_sample.append(jnp.concatenate([state_pool[l][:, 1:], proj_s[:, None, u_off:]], axis=1))

        h_p, n_p, h_s, n_s = _proj_out(o_p, pm_p, h_p, o_s, pm_s, h_s, w_out, g_ffn, l, bm_row)
        act_p, act_s = _matmul([n_p], [n_s], w_gate_up, l, n_cols=d_ff, bm=bm, bn=512,
                               mode="swiglu", up_offset=d_ff, out_dtypes=(BF16,), name="ffn_up")
        h_p, h_s = _matmul([act_p], [act_s], w_down, l, n_cols=d_model, bm=bm // 2, bn=512,
                           mode="res", extras_p=(h_p,), extras_s=(h_s,), name="ffn_down")
        if l + 1 < depth:
            h_p, n_p, h_s, n_s = _ple(h_p, pe_p, h_s, pe_s, w_ple_gate, w_ple, g_ple,
                                      g_mix[l + 1].reshape(1, d_model), l, bm_row,
                                      write_h=True, norm_dtype=BF16)
        else:
            y_p, y_s = _ple(h_p, pe_p, h_s, pe_s, w_ple_gate, w_ple, g_ple,
                            g_final.reshape(1, d_model), l, bm_row, write_h=False, norm_dtype=F32)

    return (y_p.reshape(batch, seq, d_model), y_s.reshape(dec_batch, 1, d_model),
            jnp.stack(s_prompt), jnp.stack(pool_prompt), new_state, jnp.stack(pool_sample))
```

```python
import functools

import numpy as np
import jax
import jax.numpy as jnp
from jax import lax
from jax.experimental import pallas as pl
from jax.experimental.pallas import tpu as pltpu

F32 = jnp.float32
BF16 = jnp.bfloat16

EPS = 1e-6
LOG2E = 1.4426950408889634
HGRN_DK = 128
HGRN_DV = 128
POOL_WINDOWS = (2, 4, 8, 16)
PAST_LEN = 16384
SUBLANES = 8
CHUNK = 128
LEVELS = (64, 32, 16, 8, 4, 2, 1)
HEADS_PER_STEP = 4
ROW_SUBTILE = 128

V7X_VMEM_BYTES = 64 * 1024 * 1024
VMEM_LIMIT_BYTES = V7X_VMEM_BYTES - 8 * 1024 * 1024


def _cparams(n_axes):
    return pltpu.CompilerParams(
        dimension_semantics=("arbitrary",) * n_axes, vmem_limit_bytes=VMEM_LIMIT_BYTES)


def _silu(x):
    hx = 0.5 * x
    return hx + hx * jnp.tanh(hx)


def _rms(x, g):
    return x * lax.rsqrt(jnp.mean(x * x, axis=-1, keepdims=True) + EPS) * g


def _rmsnorm_kernel(xp_ref, xs_ref, g_ref, op_ref, os_ref, *, n_prompt_tiles):
    i = pl.program_id(0)

    @pl.when(i < n_prompt_tiles)
    def _():
        op_ref[...] = _rms(xp_ref[...], g_ref[...]).astype(op_ref.dtype)

    @pl.when(i == n_prompt_tiles)
    def _():
        os_ref[...] = _rms(xs_ref[...], g_ref[...]).astype(os_ref.dtype)


def _rmsnorm(x_p, x_s, g, out_dtype, bm):
    mp, d = x_p.shape
    ms = x_s.shape[0]
    npt = mp // bm
    prow = lambda i: (jnp.minimum(i, npt - 1), 0)
    return pl.pallas_call(
        functools.partial(_rmsnorm_kernel, n_prompt_tiles=npt),
        grid=(npt + 1,),
        in_specs=[pl.BlockSpec((bm, d), prow),
                  pl.BlockSpec((ms, d), lambda i: (0, 0)),
                  pl.BlockSpec((1, d), lambda i: (0, 0))],
        out_specs=[pl.BlockSpec((bm, d), prow),
                   pl.BlockSpec((ms, d), lambda i: (0, 0))],
        out_shape=[jax.ShapeDtypeStruct((mp, d), out_dtype),
                   jax.ShapeDtypeStruct((ms, d), out_dtype)],
        compiler_params=_cparams(1),
        name="rmsnorm",
    )(x_p, x_s, g.reshape(1, d))


_N_EXTRA = {"plain": 0, "res": 1, "swiglu": 0}


def _mm_kernel(*refs, n_x, n_w, n_out, mode):
    n_e = _N_EXTRA[mode]
    pos = 0

    def take(n):
        nonlocal pos
        out = refs[pos:pos + n]
        pos += n
        return out

    xp, xs, w_refs = take(n_x), take(n_x), take(n_w)
    ep, es = take(n_e), take(n_e)
    op, osm, wbf = take(n_out), take(n_out), take(n_w)
    i = pl.program_id(1)

    @pl.when(i == 0)
    def _():
        for w_ref, wb in zip(w_refs, wbf):
            wb[...] = w_ref[...].astype(BF16)

    def compute(x_refs, extra, outs):
        cast = [xr[...].astype(BF16) for xr in x_refs]
        xb = cast[0] if n_x == 1 else jnp.concatenate(cast, axis=1)
        acc = jnp.dot(xb, wbf[0][...], preferred_element_type=F32)
        if mode == "plain":
            res = acc
        elif mode == "res":
            res = extra[0][...] + acc
        else:
            res = _silu(acc) * jnp.dot(xb, wbf[1][...], preferred_element_type=F32)
        for o_ref in outs:
            o_ref[...] = res.astype(o_ref.dtype)

    @pl.when(i == 0)
    def _():
        compute(xs, es, osm)

    @pl.when(i > 0)
    def _():
        compute(xp, ep, op)


def _matmul(xs_p, xs_s, w, layer, *, n_cols, bm, bn, mode="plain", extras_p=(), extras_s=(),
            out_dtypes=(F32,), up_offset=None, name="mm"):
    mp, ms = xs_p[0].shape[0], xs_s[0].shape[0]
    k = w.shape[1]
    assert sum(x.shape[1] for x in xs_p) == k and mp % bm == 0 and n_cols % bn == 0
    npt = mp // bm
    grid = (n_cols // bn, npt + 1)
    prow = lambda j, i: (jnp.maximum(i - 1, 0), 0)
    ptile = lambda j, i: (jnp.maximum(i - 1, 0), j)
    in_specs = [pl.BlockSpec((bm, x.shape[1]), prow) for x in xs_p]
    in_specs += [pl.BlockSpec((ms, x.shape[1]), lambda j, i: (0, 0)) for x in xs_s]
    in_specs.append(pl.BlockSpec((None, k, bn), lambda j, i: (layer, 0, j)))
    operands = list(xs_p) + list(xs_s) + [w]
    n_w = 1
    if mode == "swiglu":
        off = up_offset // bn
        in_specs.append(pl.BlockSpec((None, k, bn), lambda j, i: (layer, 0, j + off)))
        operands.append(w)
        n_w = 2
    in_specs += [pl.BlockSpec((bm, bn), ptile) for _ in extras_p]
    in_specs += [pl.BlockSpec((ms, bn), lambda j, i: (0, j)) for _ in extras_s]
    operands += list(extras_p) + list(extras_s)
    out_specs = ([pl.BlockSpec((bm, bn), ptile) for _ in out_dtypes]
                 + [pl.BlockSpec((ms, bn), lambda j, i: (0, j)) for _ in out_dtypes])
    out_shape = ([jax.ShapeDtypeStruct((mp, n_cols), dt) for dt in out_dtypes]
                 + [jax.ShapeDtypeStruct((ms, n_cols), dt) for dt in out_dtypes])
    return pl.pallas_call(
        functools.partial(_mm_kernel, n_x=len(xs_p), n_w=n_w, n_out=len(out_dtypes), mode=mode),
        grid=grid,
        in_specs=in_specs,
        out_specs=out_specs,
        out_shape=out_shape,
        scratch_shapes=[pltpu.VMEM((k, bn), BF16) for _ in range(n_w)],
        compiler_params=_cparams(2),
        name=name,
    )(*operands)


def _resident(shape, index_map):
    return pl.BlockSpec(shape, index_map, pipeline_mode=pl.Buffered(1))


def _proj_out_kernel(op_ref, pmp_ref, hp_ref, os_ref, pms_ref, hs_ref, w_ref, g_ref,
                     hp_out, np_out, hs_out, ns_out, wbf):
    i = pl.program_id(0)

    def compute(o_ref, pm_ref, h_ref, h_out, n_out):
        for r0 in range(0, h_ref.shape[0], ROW_SUBTILE):
            rows = slice(r0, r0 + ROW_SUBTILE)
            xb = jnp.concatenate([o_ref[rows, :], pm_ref[rows, :]], axis=1)
            hn = h_ref[rows, :] + jnp.dot(xb, wbf[...], preferred_element_type=F32)
            h_out[rows, :] = hn
            n_out[rows, :] = _rms(hn, g_ref[...]).astype(n_out.dtype)

    @pl.when(i == 0)
    def _():
        wbf[...] = w_ref[...].astype(BF16)
        compute(os_ref, pms_ref, hs_ref, hs_out, ns_out)

    @pl.when(i > 0)
    def _():
        compute(op_ref, pmp_ref, hp_ref, hp_out, np_out)


def _proj_out(o_p, pm_p, h_p, o_s, pm_s, h_s, w_out, g_ffn, layer, bm):
    mp, d = h_p.shape
    ms = h_s.shape[0]
    ko, kp = o_p.shape[1], pm_p.shape[1]
    prow = lambda i: (jnp.maximum(i - 1, 0), 0)
    srow = lambda i: (0, 0)
    return pl.pallas_call(
        _proj_out_kernel,
        grid=(mp // bm + 1,),
        in_specs=[pl.BlockSpec((bm, ko), prow), pl.BlockSpec((bm, kp), prow), pl.BlockSpec((bm, d), prow),
                  pl.BlockSpec((ms, ko), srow), pl.BlockSpec((ms, kp), srow), pl.BlockSpec((ms, d), srow),
                  _resident((None, ko + kp, d), lambda i: (layer, 0, 0)),
                  pl.BlockSpec((None, 1, d), lambda i: (layer, 0, 0))],
        out_specs=[pl.BlockSpec((bm, d), prow), pl.BlockSpec((bm, d), prow),
                   pl.BlockSpec((ms, d), srow), pl.BlockSpec((ms, d), srow)],
        out_shape=[jax.ShapeDtypeStruct((mp, d), F32), jax.ShapeDtypeStruct((mp, d), BF16),
                   jax.ShapeDtypeStruct((ms, d), F32), jax.ShapeDtypeStruct((ms, d), BF16)],
        scratch_shapes=[pltpu.VMEM((ko + kp, d), BF16)],
        compiler_params=_cparams(1),
        name="proj_out",
    )(o_p, pm_p, h_p, o_s, pm_s, h_s, w_out, g_ffn.reshape(-1, 1, d))


def _ple_kernel(*refs, write_h):
    hp_ref, pp_ref, hs_ref, ps_ref, wg_ref, wp_ref, ge_ref, gn_ref = refs[:8]
    outs = refs[8:-2]
    wgb, wpb = refs[-2:]
    if write_h:
        hp_out, np_out, hs_out, ns_out = outs
    else:
        (np_out, ns_out), hp_out, hs_out = outs, None, None
    i = pl.program_id(0)

    def compute(h_ref, p_ref, h_out, n_out):
        for r0 in range(0, h_ref.shape[0], ROW_SUBTILE):
            rows = slice(r0, r0 + ROW_SUBTILE)
            hv = h_ref[rows, :]
            gate = jax.nn.sigmoid(jnp.dot(hv.astype(BF16), wgb[...], preferred_element_type=F32))
            e = _rms(jnp.dot(p_ref[rows, :].astype(BF16), wpb[...], preferred_element_type=F32),
                     ge_ref[...])
            hn = hv + gate * e
            if write_h:
                h_out[rows, :] = hn
            n_out[rows, :] = _rms(hn, gn_ref[...]).astype(n_out.dtype)

    @pl.when(i == 0)
    def _():
        wgb[...] = wg_ref[...].astype(BF16)
        wpb[...] = wp_ref[...].astype(BF16)
        compute(hs_ref, ps_ref, hs_out, ns_out)

    @pl.when(i > 0)
    def _():
        compute(hp_ref, pp_ref, hp_out, np_out)


def _ple(h_p, pe_p, h_s, pe_s, w_gate, w_ple, g_ple, g_next, layer, bm, *, write_h, norm_dtype):
    mp, d = h_p.shape
    ms = h_s.shape[0]
    kp = pe_p.shape[2]
    prow = lambda i: (jnp.maximum(i - 1, 0), 0)
    srow = lambda i: (0, 0)
    n_specs = [pl.BlockSpec((bm, d), prow), pl.BlockSpec((ms, d), srow)]
    n_shapes = [jax.ShapeDtypeStruct((mp, d), norm_dtype), jax.ShapeDtypeStruct((ms, d), norm_dtype)]
    if write_h:
        out_specs = [n_specs[0], n_specs[0], n_specs[1], n_specs[1]]
        out_shape = [jax.ShapeDtypeStruct((mp, d), F32), n_shapes[0],
                     jax.ShapeDtypeStruct((ms, d), F32), n_shapes[1]]
    else:
        out_specs, out_shape = n_specs, n_shapes
    return pl.pallas_call(
        functools.partial(_ple_kernel, write_h=write_h),
        grid=(mp // bm + 1,),
        in_specs=[pl.BlockSpec((bm, d), prow),
                  pl.BlockSpec((None, bm, kp), lambda i: (layer, jnp.maximum(i - 1, 0), 0)),
                  pl.BlockSpec((ms, d), srow),
                  pl.BlockSpec((None, ms, kp), lambda i: (layer, 0, 0)),
                  _resident((None, d, d), lambda i: (layer, 0, 0)),
                  _resident((None, kp, d), lambda i: (layer, 0, 0)),
                  pl.BlockSpec((None, 1, d), lambda i: (layer, 0, 0)),
                  pl.BlockSpec((1, d), lambda i: (0, 0))],
        out_specs=out_specs,
        out_shape=out_shape,
        scratch_shapes=[pltpu.VMEM((d, d), BF16), pltpu.VMEM((kp, d), BF16)],
        compiler_params=_cparams(1),
        name="ple",
    )(h_p, pe_p, h_s, pe_s, w_gate, w_ple, g_ple.reshape(-1, 1, d), g_next)


def _chunk_constants():
    c = CHUNK
    t = np.arange(c)[:, None]
    s = np.arange(c)[None, :]
    masks, signs = [], []
    for h in LEVELS:
        same = (t // (2 * h)) == (s // (2 * h))
        upper = (t % (2 * h)) >= h
        masks.append((same & upper & ((s % (2 * h)) < h)).astype(np.float32))
        signs.append(np.broadcast_to(np.where(upper, LOG2E, -LOG2E),
                                     (c, HEADS_PER_STEP * HGRN_DK)).astype(np.float32))
    tri = (s <= t).astype(np.float32)
    return (jnp.asarray(tri, dtype=BF16), jnp.asarray(np.stack(masks), dtype=BF16),
            jnp.asarray(np.stack(signs), dtype=F32))


def _lower_bound(lb_ref, layer):
    lg = lb_ref[...]
    e = jnp.exp(lg - jnp.max(lg, axis=0, keepdims=True))
    s = e / jnp.sum(e, axis=0, keepdims=True)
    c = s[0:1, :]
    for r in range(1, layer + 1):
        c = c + s[r:r + 1, :]
    return c - s[0:1, :]


def _gates(q_raw, f_raw, lb):
    q = _silu(q_raw) * (HGRN_DK ** -0.5)
    fg = lb + (1.0 - lb) * jax.nn.sigmoid(f_raw)
    return q, 1.0 - fg, fg


def _split2(x):
    hi = x.astype(BF16)
    lo = (x - hi.astype(F32)).astype(BF16)
    return jnp.concatenate([hi, lo], axis=1)


def _boundary_rows(b, h):
    n_grp = CHUNK // SUBLANES

    def bcast(grp, sub):
        r = grp * SUBLANES + sub
        return jnp.broadcast_to(b[r:r + 1, :], (SUBLANES, b.shape[1]))

    sub_id = lax.broadcasted_iota(jnp.int32, (SUBLANES, b.shape[1]), 0)
    pieces = []
    for grp in range(n_grp):
        if h >= SUBLANES:
            per_blk = 2 * h // SUBLANES
            pieces.append(bcast((grp // per_blk) * per_blk + per_blk // 2 - 1, SUBLANES - 1))
        else:
            piece = bcast(grp, h - 1)
            for blk in range(1, SUBLANES // (2 * h)):
                piece = jnp.where(sub_id < blk * 2 * h, piece, bcast(grp, blk * 2 * h + h - 1))
            pieces.append(piece)
    return jnp.concatenate(pieces, axis=0)


_NT = (((1,), (1,)), ((), ()))


def _hgrn_chunk(q_raw, f_raw, v, lb, tri_ref, mask_ref, sign_ref, st_ref):
    c, w = CHUNK, q_raw.shape[1]
    heads = [slice(i * HGRN_DK, (i + 1) * HGRN_DK) for i in range(w // HGRN_DK)]
    q, k, fg = _gates(q_raw, f_raw, lb)
    b2 = jnp.dot(tri_ref[...], _split2(jnp.log(fg)), preferred_element_type=F32)
    b = b2[:, :w] + b2[:, w:]
    q16, k16, v16 = q.astype(BF16), k.astype(BF16), v.astype(BF16)
    a = [jnp.zeros((c, c), BF16) for _ in heads]
    for li, h in enumerate(LEVELS):
        if h == 1:
            e = jnp.where(sign_ref[li] > 0.0, fg, 1.0).astype(BF16)
        else:
            e = jnp.exp2((b - _boundary_rows(b, h)) * sign_ref[li]).astype(BF16)
        qe, ke = q16 * e, k16 * e
        for i, cs in enumerate(heads):
            ah = lax.dot_general(qe[:, cs], ke[:, cs], _NT, preferred_element_type=F32)
            a[i] = a[i] + ah.astype(BF16) * mask_ref[li]
    qk = q * k
    qb = (q * jnp.exp(b)).astype(BF16)
    bl = b[c - 1:c, :]
    kd = (k * jnp.exp(bl - b)).astype(BF16)
    el = jnp.exp(bl)
    outs = []
    for i, cs in enumerate(heads):
        st = st_ref[i]
        o = jnp.dot(a[i], v16[:, cs], preferred_element_type=F32)
        o = o + jnp.sum(qk[:, cs], axis=1, keepdims=True) * v[:, cs]
        o = o + lax.dot_general(qb[:, cs], st.astype(BF16), _NT, preferred_element_type=F32)
        st_ref[i] = st * el[:, cs] + jnp.dot(v[:, cs].T.astype(BF16), kd[:, cs],
                                             preferred_element_type=F32)
        outs.append(o)
    return outs


def _hgrn_prompt_kernel(q_ref, f_ref, v_ref, g_ref, lb_ref, gh_ref, tri_ref, mask_ref, sign_ref,
                        o_ref, s_ref, st_ref, *, layer, n_chunks):
    st_ref[...] = jnp.zeros_like(st_ref)
    lb = _lower_bound(lb_ref, layer)
    gh = gh_ref[layer:layer + 1, :]

    def body(ci, carry):
        rows = pl.ds(pl.multiple_of(ci * CHUNK, CHUNK), CHUNK)
        outs = _hgrn_chunk(q_ref[rows, :], f_ref[rows, :], v_ref[rows, :], lb, tri_ref, mask_ref,
                           sign_ref, st_ref)
        o = jnp.concatenate([_rms(oh, gh) for oh in outs], axis=1)
        o_ref[rows, :] = (o * _silu(g_ref[rows, :])).astype(o_ref.dtype)
        return carry

    lax.fori_loop(0, n_chunks, body, 0, unroll=2)
    for hh in range(HEADS_PER_STEP):
        s_ref[hh] = st_ref[hh].T


def _hgrn_prompt(proj, lb_logits, g_head, consts, layer, batch, seq, heads):
    hp = HEADS_PER_STEP
    n_hp = heads // hp
    wid = hp * HGRN_DK
    col = lambda grp: (lambda b, p: (b, grp * n_hp + p))
    depth = lb_logits.shape[0]
    tri, masks, signs = consts
    return pl.pallas_call(
        functools.partial(_hgrn_prompt_kernel, layer=layer, n_chunks=seq // CHUNK),
        grid=(batch, n_hp),
        in_specs=[pl.BlockSpec((seq, wid), col(0)),
                  pl.BlockSpec((seq, wid), col(1)),
                  pl.BlockSpec((seq, wid), col(2)),
                  pl.BlockSpec((seq, wid), col(3)),
                  pl.BlockSpec((depth, wid), lambda b, p: (0, p)),
                  pl.BlockSpec((depth, HGRN_DV), lambda b, p: (0, 0)),
                  pl.BlockSpec(tri.shape, lambda b, p: (0, 0)),
                  pl.BlockSpec(masks.shape, lambda b, p: (0, 0, 0)),
                  pl.BlockSpec(signs.shape, lambda b, p: (0, 0, 0))],
        out_specs=[pl.BlockSpec((seq, wid), lambda b, p: (b, p)),
                   pl.BlockSpec((None, hp, HGRN_DK, HGRN_DV), lambda b, p: (b, p, 0, 0))],
        out_shape=[jax.ShapeDtypeStruct((batch * seq, heads * HGRN_DV), BF16),
                   jax.ShapeDtypeStruct((batch, heads, HGRN_DK, HGRN_DV), F32)],
        scratch_shapes=[pltpu.VMEM((hp, HGRN_DV, HGRN_DK), F32)],
        compiler_params=_cparams(2),
        name="hgrn_prompt",
    )(proj, proj, proj, proj, lb_logits, g_head, tri, masks, signs)


def _hgrn_sample_kernel(q_ref, f_ref, v_ref, g_ref, lb_ref, gh_ref, s_ref, *rest, layer, dec_batch):
    o_ref, sn_ref, orow_ref = rest[-3:]
    lb = _lower_bound(lb_ref, layer)
    q, k, fg = _gates(q_ref[...], f_ref[...], lb)
    qt, kt, ft = q.T, k.T, fg.T
    for bi in range(dec_batch):
        s_new = ft[:, bi:bi + 1] * s_ref[bi] + kt[:, bi:bi + 1] * v_ref[bi:bi + 1, :]
        sn_ref[bi] = s_new
        orow_ref[bi:bi + 1, :] = jnp.sum(qt[:, bi:bi + 1] * s_new, axis=0, keepdims=True)
    gh = gh_ref[layer:layer + 1, :]
    o_ref[...] = (_rms(orow_ref[...], gh) * _silu(g_ref[...])).astype(o_ref.dtype)


def _hgrn_sample(proj, state, lb_logits, g_head, layer, dec_batch, heads, new_state=None):
    col = lambda grp: (lambda h: (0, grp * heads + h))
    depth = lb_logits.shape[0]
    slab = pl.BlockSpec((None, dec_batch, None, HGRN_DK, HGRN_DV), lambda h: (layer, 0, h, 0, 0))
    in_specs = [pl.BlockSpec((dec_batch, HGRN_DK), col(0)),
                pl.BlockSpec((dec_batch, HGRN_DK), col(1)),
                pl.BlockSpec((dec_batch, HGRN_DV), col(2)),
                pl.BlockSpec((dec_batch, HGRN_DV), col(3)),
                pl.BlockSpec((depth, HGRN_DK), lambda h: (0, h)),
                pl.BlockSpec((depth, HGRN_DV), lambda h: (0, 0)),
                slab]
    operands = [proj, proj, proj, proj, lb_logits, g_head, state]
    aliases = {}
    if new_state is not None:
        in_specs.append(pl.BlockSpec(memory_space=pl.ANY))
        operands.append(new_state)
        aliases = {len(operands) - 1: 1}
    return pl.pallas_call(
        functools.partial(_hgrn_sample_kernel, layer=layer, dec_batch=dec_batch),
        grid=(heads,),
        in_specs=in_specs,
        out_specs=[pl.BlockSpec((dec_batch, HGRN_DV), lambda h: (0, h)), slab],
        out_shape=[jax.ShapeDtypeStruct((dec_batch, heads * HGRN_DV), BF16),
                   jax.ShapeDtypeStruct(state.shape, F32)],
        scratch_shapes=[pltpu.VMEM((dec_batch, HGRN_DV), F32)],
        input_output_aliases=aliases,
        compiler_params=_cparams(1),
        name="hgrn_sample",
    )(*operands)


POOL_CARRY = 16


def _pool_mix(pooled, wp_ref, ps_ref, o_ref, d_pg):
    for gi in range(len(POOL_WINDOWS)):
        cs = slice(gi * d_pg, (gi + 1) * d_pg)
        pm = jnp.dot(pooled[gi].astype(BF16), wp_ref[gi].astype(BF16), preferred_element_type=F32)
        o_ref[:, cs] = (pm * ps_ref[:, cs]).astype(o_ref.dtype)


def _pool_prompt_kernel(u_ref, wp_ref, ps_ref, o_ref, ext_ref, *, tt, d_pg):
    t = pl.program_id(1)

    @pl.when(t == 0)
    def _():
        ext_ref[0:POOL_CARRY, :] = jnp.zeros((POOL_CARRY, ext_ref.shape[1]), F32)

    @pl.when(t > 0)
    def _():
        ext_ref[0:POOL_CARRY, :] = ext_ref[tt:tt + POOL_CARRY, :]

    ext_ref[POOL_CARRY:POOL_CARRY + tt, :] = u_ref[...]
    pos = t * tt + lax.broadcasted_iota(jnp.int32, (tt, 1), 0)
    pooled = []
    for gi, w in enumerate(POOL_WINDOWS):
        cs = slice(gi * d_pg, (gi + 1) * d_pg)
        s = ext_ref[:, cs]
        span = 1
        while span < w:
            s = s + pltpu.roll(s, span, 0)
            span *= 2
        inv_cnt = 1.0 / jnp.minimum(pos + 1, w).astype(F32)
        pooled.append(s[POOL_CARRY:, :] * inv_cnt - u_ref[:, cs])
    _pool_mix(pooled, wp_ref, ps_ref, o_ref, d_pg)


def _pool_prompt(proj, w_pool, pool_scale, layer, batch, seq, u_col_blk, tt):
    d_pool = pool_scale.shape[1]
    n_pool, d_pg = w_pool.shape[1], w_pool.shape[2]
    nt = seq // tt
    return pl.pallas_call(
        functools.partial(_pool_prompt_kernel, tt=tt, d_pg=d_pg),
        grid=(batch, nt),
        in_specs=[pl.BlockSpec((tt, d_pool), lambda b, t: (b * nt + t, u_col_blk)),
                  pl.BlockSpec((None, n_pool, d_pg, d_pg), lambda b, t: (layer, 0, 0, 0)),
                  pl.BlockSpec((None, 1, d_pool), lambda b, t: (layer, 0, 0))],
        out_specs=pl.BlockSpec((tt, d_pool), lambda b, t: (b * nt + t, 0)),
        out_shape=jax.ShapeDtypeStruct((batch * seq, d_pool), BF16),
        scratch_shapes=[pltpu.VMEM((tt + POOL_CARRY, d_pool), F32)],
        compiler_params=_cparams(2),
        name="pool_prompt",
    )(proj, w_pool, pool_scale.reshape(-1, 1, d_pool))


def _pool_sample_kernel(u_ref, buf_ref, wp_ref, ps_ref, o_ref, *, d_pg, n_buf):
    pooled = []
    for gi, w in enumerate(POOL_WINDOWS):
        cs = slice(gi * d_pg, (gi + 1) * d_pg)
        u = u_ref[:, cs]
        s = u
        for j in range(1, w):
            s = s + buf_ref[n_buf - j, :, cs]
        cnt = float(min(PAST_LEN + 1, w))
        pooled.append(s / cnt - u)
    _pool_mix(pooled, wp_ref, ps_ref, o_ref, d_pg)


def _pool_sample(proj, buf_t, w_pool, pool_scale, layer, u_col_blk, dec_batch):
    d_pool = pool_scale.shape[1]
    n_pool, d_pg = w_pool.shape[1], w_pool.shape[2]
    n_buf = buf_t.shape[0]
    return pl.pallas_call(
        functools.partial(_pool_sample_kernel, d_pg=d_pg, n_buf=n_buf),
        grid=(1,),
        in_specs=[pl.BlockSpec((dec_batch, d_pool), lambda i: (0, u_col_blk)),
                  pl.BlockSpec(buf_t.shape, lambda i: (0, 0, 0)),
                  pl.BlockSpec((None, n_pool, d_pg, d_pg), lambda i: (layer, 0, 0, 0)),
                  pl.BlockSpec((None, 1, d_pool), lambda i: (layer, 0, 0))],
        out_specs=pl.BlockSpec((dec_batch, d_pool), lambda i: (0, 0)),
        out_shape=jax.ShapeDtypeStruct((dec_batch, d_pool), BF16),
        compiler_params=_cparams(1),
        name="pool_sample",
    )(proj, buf_t, w_pool, pool_scale.reshape(-1, 1, d_pool))


def kernel(x_prompt, x_sample, state_hgrn, state_pool, p_prompt, p_sample, g_mix, w_in, lb_logits,
           g_head, w_pool, pool_scale, w_out, g_ffn, w_gate_up, w_down, w_ple, g_ple, w_ple_gate,
           g_final):
    batch, seq, d_model = x_prompt.shape
    dec_batch = x_sample.shape[0]
    depth = w_in.shape[0]
    heads = lb_logits.shape[1] // HGRN_DK
    d_hgrn = heads * HGRN_DV
    d_pool = pool_scale.shape[1]
    d_ff = w_down.shape[1]
    pool_buf = state_pool.shape[2]
    mp = batch * seq
    assert x_sample.shape[1] == 1 and seq % CHUNK == 0 and heads % HEADS_PER_STEP == 0
    assert dec_batch % ROW_SUBTILE == 0
    assert w_in.shape[2] == 2 * heads * HGRN_DK + 2 * d_hgrn + d_pool

    bm = 1024
    bm_row = 512
    u_col_blk = (2 * heads * HGRN_DK + 2 * d_hgrn) // d_pool
    consts = _chunk_constants()

    h_p = x_prompt.reshape(mp, d_model)
    h_s = x_sample.reshape(dec_batch, d_model)
    pe_p = p_prompt.reshape(depth, mp, -1)
    pe_s = p_sample.reshape(depth, dec_batch, -1)

    s_prompt, pool_prompt, pool_sample = [], [], []
    new_state = None
    n_p, n_s = _rmsnorm(h_p, h_s, g_mix[0], BF16, bm)
    for l in range(depth):
        proj_p, proj_s = _matmul([n_p], [n_s], w_in, l, n_cols=w_in.shape[2], bm=bm, bn=1024,
                                 name="proj_in")

        o_p, st_p = _hgrn_prompt(proj_p, lb_logits, g_head, consts, l, batch, seq, heads)
        o_s, new_state = _hgrn_sample(proj_s, state_hgrn, lb_logits, g_head, l, dec_batch, heads,
                                      new_state)
        pm_p = _pool_prompt(proj_p, w_pool, pool_scale, l, batch, seq, u_col_blk, tt=512)
        buf_t = jnp.transpose(state_pool[l], (1, 0, 2))
        pm_s = _pool_sample(proj_s, buf_t, w_pool, pool_scale, l, u_col_blk, dec_batch)
        s_prompt.append(st_p)
        u_off = u_col_blk * d_pool
        pool_prompt.append(proj_p.reshape(batch, seq, -1)[:, seq - pool_buf:, u_off:])
        pool_sample.append(jnp.concatenate([state_pool[l][:, 1:], proj_s[:, None, u_off:]], axis=1))

        h_p, n_p, h_s, n_s = _proj_out(o_p, pm_p, h_p, o_s, pm_s, h_s, w_out, g_ffn, l, bm_row)
        act_p, act_s = _matmul([n_p], [n_s], w_gate_up, l, n_cols=d_ff, bm=bm, bn=512,
                               mode="swiglu", up_offset=d_ff, out_dtypes=(BF16,), name="ffn_up")
        h_p, h_s = _matmul([act_p], [act_s], w_down, l, n_cols=d_model, bm=bm // 2, bn=512,
                           mode="res", extras_p=(h_p,), extras_s=(h_s,), name="ffn_down")
        if l + 1 < depth:
            h_p, n_p, h_s, n_s = _ple(h_p, pe_p, h_s, pe_s, w_ple_gate, w_ple, g_ple,
                                      g_mix[l + 1].reshape(1, d_model), l, bm_row,
                                      write_h=True, norm_dtype=BF16)
        else:
            y_p, y_s = _ple(h_p, pe_p, h_s, pe_s, w_ple_gate, w_ple, g_ple,
                            g_final.reshape(1, d_model), l, bm_row, write_h=False, norm_dtype=F32)

    return (y_p.reshape(batch, seq, d_model), y_s.reshape(dec_batch, 1, d_model),
            jnp.stack(s_prompt), jnp.stack(pool_prompt), new_state, jnp.stack(pool_sample))
```

```python
import functools

import numpy as np
import jax
import jax.numpy as jnp
from jax import lax
from jax.experimental import pallas as pl
from jax.experimental.pallas import tpu as pltpu

F32 = jnp.float32
BF16 = jnp.bfloat16

EPS = 1e-6
LOG2E = 1.4426950408889634
HGRN_DK = 128
HGRN_DV = 128
POOL_WINDOWS = (2, 4, 8, 16)
PAST_LEN = 16384
SUBLANES = 8
CHUNK = 128
LEVELS = (64, 32, 16, 8, 4, 2, 1)
HEADS_PER_STEP = 4
ROW_SUBTILE = 128

V7X_VMEM_BYTES = 64 * 1024 * 1024
VMEM_LIMIT_BYTES = V7X_VMEM_BYTES - 8 * 1024 * 1024


def _cparams(n_axes):
    return pltpu.CompilerParams(
        dimension_semantics=("arbitrary",) * n_axes, vmem_limit_bytes=VMEM_LIMIT_BYTES)


def _silu(x):
    hx = 0.5 * x
    return hx + hx * jnp.tanh(hx)


def _rms(x, g):
    return x * lax.rsqrt(jnp.mean(x * x, axis=-1, keepdims=True) + EPS) * g


_N_EXTRA = {"plain": 0, "res": 1, "swiglu": 0}


def _mm_kernel(*refs, n_x, n_w, n_out, mode, norm_x):
    n_e = _N_EXTRA[mode]
    pos = 0

    def take(n):
        nonlocal pos
        out = refs[pos:pos + n]
        pos += n
        return out

    xp, xs, w_refs = take(n_x), take(n_x), take(n_w)
    ep, es, gain = take(n_e), take(n_e), take(1 if norm_x else 0)
    op, osm, wbf = take(n_out), take(n_out), take(n_w)
    i = pl.program_id(1)

    @pl.when(i == 0)
    def _():
        for w_ref, wb in zip(w_refs, wbf):
            wb[...] = w_ref[...].astype(BF16)

    def compute(x_refs, extra, outs):
        if norm_x:
            cast = [_rms(x_refs[0][...], gain[0][...]).astype(BF16)]
        else:
            cast = [xr[...].astype(BF16) for xr in x_refs]
        xb = cast[0] if n_x == 1 else jnp.concatenate(cast, axis=1)
        acc = jnp.dot(xb, wbf[0][...], preferred_element_type=F32)
        if mode == "plain":
            res = acc
        elif mode == "res":
            res = extra[0][...] + acc
        else:
            res = _silu(acc) * jnp.dot(xb, wbf[1][...], preferred_element_type=F32)
        for o_ref in outs:
            o_ref[...] = res.astype(o_ref.dtype)

    @pl.when(i == 0)
    def _():
        compute(xs, es, osm)

    @pl.when(i > 0)
    def _():
        compute(xp, ep, op)


def _matmul(xs_p, xs_s, w, layer, *, n_cols, bm, bn, mode="plain", extras_p=(), extras_s=(),
            out_dtypes=(F32,), up_offset=None, norm_gain=None, name="mm"):
    mp, ms = xs_p[0].shape[0], xs_s[0].shape[0]
    k = w.shape[1]
    assert sum(x.shape[1] for x in xs_p) == k and mp % bm == 0 and n_cols % bn == 0
    npt = mp // bm
    grid = (n_cols // bn, npt + 1)
    prow = lambda j, i: (jnp.maximum(i - 1, 0), 0)
    ptile = lambda j, i: (jnp.maximum(i - 1, 0), j)
    in_specs = [pl.BlockSpec((bm, x.shape[1]), prow) for x in xs_p]
    in_specs += [pl.BlockSpec((ms, x.shape[1]), lambda j, i: (0, 0)) for x in xs_s]
    in_specs.append(pl.BlockSpec((None, k, bn), lambda j, i: (layer, 0, j)))
    operands = list(xs_p) + list(xs_s) + [w]
    n_w = 1
    if mode == "swiglu":
        off = up_offset // bn
        in_specs.append(pl.BlockSpec((None, k, bn), lambda j, i: (layer, 0, j + off)))
        operands.append(w)
        n_w = 2
    in_specs += [pl.BlockSpec((bm, bn), ptile) for _ in extras_p]
    in_specs += [pl.BlockSpec((ms, bn), lambda j, i: (0, j)) for _ in extras_s]
    operands += list(extras_p) + list(extras_s)
    if norm_gain is not None:
        assert len(xs_p) == 1
        in_specs.append(pl.BlockSpec((1, k), lambda j, i: (0, 0)))
        operands.append(norm_gain)
    out_specs = ([pl.BlockSpec((bm, bn), ptile) for _ in out_dtypes]
                 + [pl.BlockSpec((ms, bn), lambda j, i: (0, j)) for _ in out_dtypes])
    out_shape = ([jax.ShapeDtypeStruct((mp, n_cols), dt) for dt in out_dtypes]
                 + [jax.ShapeDtypeStruct((ms, n_cols), dt) for dt in out_dtypes])
    return pl.pallas_call(
        functools.partial(_mm_kernel, n_x=len(xs_p), n_w=n_w, n_out=len(out_dtypes), mode=mode,
                          norm_x=norm_gain is not None),
        grid=grid,
        in_specs=in_specs,
        out_specs=out_specs,
        out_shape=out_shape,
        scratch_shapes=[pltpu.VMEM((k, bn), BF16) for _ in range(n_w)],
        compiler_params=_cparams(2),
        name=name,
    )(*operands)


def _resident(shape, index_map):
    return pl.BlockSpec(shape, index_map, pipeline_mode=pl.Buffered(1))


def _proj_out_kernel(op_ref, pmp_ref, hp_ref, os_ref, pms_ref, hs_ref, w_ref, g_ref,
                     hp_out, np_out, hs_out, ns_out, wbf):
    i = pl.program_id(0)

    def compute(o_ref, pm_ref, h_ref, h_out, n_out):
        for r0 in range(0, h_ref.shape[0], ROW_SUBTILE):
            rows = slice(r0, r0 + ROW_SUBTILE)
            xb = jnp.concatenate([o_ref[rows, :], pm_ref[rows, :]], axis=1)
            hn = h_ref[rows, :] + jnp.dot(xb, wbf[...], preferred_element_type=F32)
            h_out[rows, :] = hn
            n_out[rows, :] = _rms(hn, g_ref[...]).astype(n_out.dtype)

    @pl.when(i == 0)
    def _():
        wbf[...] = w_ref[...].astype(BF16)
        compute(os_ref, pms_ref, hs_ref, hs_out, ns_out)

    @pl.when(i > 0)
    def _():
        compute(op_ref, pmp_ref, hp_ref, hp_out, np_out)


def _proj_out(o_p, pm_p, h_p, o_s, pm_s, h_s, w_out, g_ffn, layer, bm):
    mp, d = h_p.shape
    ms = h_s.shape[0]
    ko, kp = o_p.shape[1], pm_p.shape[1]
    prow = lambda i: (jnp.maximum(i - 1, 0), 0)
    srow = lambda i: (0, 0)
    return pl.pallas_call(
        _proj_out_kernel,
        grid=(mp // bm + 1,),
        in_specs=[pl.BlockSpec((bm, ko), prow), pl.BlockSpec((bm, kp), prow), pl.BlockSpec((bm, d), prow),
                  pl.BlockSpec((ms, ko), srow), pl.BlockSpec((ms, kp), srow), pl.BlockSpec((ms, d), srow),
                  _resident((None, ko + kp, d), lambda i: (layer, 0, 0)),
                  pl.BlockSpec((None, 1, d), lambda i: (layer, 0, 0))],
        out_specs=[pl.BlockSpec((bm, d), prow), pl.BlockSpec((bm, d), prow),
                   pl.BlockSpec((ms, d), srow), pl.BlockSpec((ms, d), srow)],
        out_shape=[jax.ShapeDtypeStruct((mp, d), F32), jax.ShapeDtypeStruct((mp, d), BF16),
                   jax.ShapeDtypeStruct((ms, d), F32), jax.ShapeDtypeStruct((ms, d), BF16)],
        scratch_shapes=[pltpu.VMEM((ko + kp, d), BF16)],
        compiler_params=_cparams(1),
        name="proj_out",
    )(o_p, pm_p, h_p, o_s, pm_s, h_s, w_out, g_ffn.reshape(-1, 1, d))


def _ple_kernel(*refs, write_h):
    hp_ref, pp_ref, hs_ref, ps_ref, wg_ref, wp_ref, ge_ref, gn_ref = refs[:8]
    outs = refs[8:-2]
    wgb, wpb = refs[-2:]
    if write_h:
        hp_out, np_out, hs_out, ns_out = outs
    else:
        (np_out, ns_out), hp_out, hs_out = outs, None, None
    i = pl.program_id(0)

    def compute(h_ref, p_ref, h_out, n_out):
        for r0 in range(0, h_ref.shape[0], ROW_SUBTILE):
            rows = slice(r0, r0 + ROW_SUBTILE)
            hv = h_ref[rows, :]
            gate = jax.nn.sigmoid(jnp.dot(hv.astype(BF16), wgb[...], preferred_element_type=F32))
            e = _rms(jnp.dot(p_ref[rows, :].astype(BF16), wpb[...], preferred_element_type=F32),
                     ge_ref[...])
            hn = hv + gate * e
            if write_h:
                h_out[rows, :] = hn
            n_out[rows, :] = _rms(hn, gn_ref[...]).astype(n_out.dtype)

    @pl.when(i == 0)
    def _():
        wgb[...] = wg_ref[...].astype(BF16)
        wpb[...] = wp_ref[...].astype(BF16)
        compute(hs_ref, ps_ref, hs_out, ns_out)

    @pl.when(i > 0)
    def _():
        compute(hp_ref, pp_ref, hp_out, np_out)


def _ple(h_p, pe_p, h_s, pe_s, w_gate, w_ple, g_ple, g_next, layer, bm, *, write_h, norm_dtype):
    mp, d = h_p.shape
    ms = h_s.shape[0]
    kp = pe_p.shape[2]
    prow = lambda i: (jnp.maximum(i - 1, 0), 0)
    srow = lambda i: (0, 0)
    n_specs = [pl.BlockSpec((bm, d), prow), pl.BlockSpec((ms, d), srow)]
    n_shapes = [jax.ShapeDtypeStruct((mp, d), norm_dtype), jax.ShapeDtypeStruct((ms, d), norm_dtype)]
    if write_h:
        out_specs = [n_specs[0], n_specs[0], n_specs[1], n_specs[1]]
        out_shape = [jax.ShapeDtypeStruct((mp, d), F32), n_shapes[0],
                     jax.ShapeDtypeStruct((ms, d), F32), n_shapes[1]]
    else:
        out_specs, out_shape = n_specs, n_shapes
    return pl.pallas_call(
        functools.partial(_ple_kernel, write_h=write_h),
        grid=(mp // bm + 1,),
        in_specs=[pl.BlockSpec((bm, d), prow),
                  pl.BlockSpec((None, bm, kp), lambda i: (layer, jnp.maximum(i - 1, 0), 0)),
                  pl.BlockSpec((ms, d), srow),
                  pl.BlockSpec((None, ms, kp), lambda i: (layer, 0, 0)),
                  _resident((None, d, d), lambda i: (layer, 0, 0)),
                  _resident((None, kp, d), lambda i: (layer, 0, 0)),
                  pl.BlockSpec((None, 1, d), lambda i: (layer, 0, 0)),
                  pl.BlockSpec((1, d), lambda i: (0, 0))],
        out_specs=out_specs,
        out_shape=out_shape,
        scratch_shapes=[pltpu.VMEM((d, d), BF16), pltpu.VMEM((kp, d), BF16)],
        compiler_params=_cparams(1),
        name="ple",
    )(h_p, pe_p, h_s, pe_s, w_gate, w_ple, g_ple.reshape(-1, 1, d), g_next)


def _chunk_constants():
    c = CHUNK
    t = np.arange(c)[:, None]
    s = np.arange(c)[None, :]
    masks, signs = [], []
    for h in LEVELS:
        same = (t // (2 * h)) == (s // (2 * h))
        upper = (t % (2 * h)) >= h
        masks.append((same & upper & ((s % (2 * h)) < h)).astype(np.float32))
        signs.append(np.broadcast_to(np.where(upper, LOG2E, -LOG2E),
                                     (c, HEADS_PER_STEP * HGRN_DK)).astype(np.float32))
    tri = (s <= t).astype(np.float32)
    return (jnp.asarray(tri, dtype=BF16), jnp.asarray(np.stack(masks), dtype=BF16),
            jnp.asarray(np.stack(signs), dtype=F32))


def _lower_bound(lb_ref, layer):
    lg = lb_ref[...]
    e = jnp.exp(lg - jnp.max(lg, axis=0, keepdims=True))
    s = e / jnp.sum(e, axis=0, keepdims=True)
    c = s[0:1, :]
    for r in range(1, layer + 1):
        c = c + s[r:r + 1, :]
    return c - s[0:1, :]


def _gates(q_raw, f_raw, lb):
    q = _silu(q_raw) * (HGRN_DK ** -0.5)
    fg = lb + (1.0 - lb) * jax.nn.sigmoid(f_raw)
    return q, 1.0 - fg, fg


def _split2(x):
    hi = x.astype(BF16)
    lo = (x - hi.astype(F32)).astype(BF16)
    return jnp.concatenate([hi, lo], axis=1)


def _boundary_rows(b, h):
    n_grp = CHUNK // SUBLANES

    def bcast(grp, sub):
        r = grp * SUBLANES + sub
        return jnp.broadcast_to(b[r:r + 1, :], (SUBLANES, b.shape[1]))

    sub_id = lax.broadcasted_iota(jnp.int32, (SUBLANES, b.shape[1]), 0)
    pieces = []
    for grp in range(n_grp):
        if h >= SUBLANES:
            per_blk = 2 * h // SUBLANES
            pieces.append(bcast((grp // per_blk) * per_blk + per_blk // 2 - 1, SUBLANES - 1))
        else:
            piece = bcast(grp, h - 1)
            for blk in range(1, SUBLANES // (2 * h)):
                piece = jnp.where(sub_id < blk * 2 * h, piece, bcast(grp, blk * 2 * h + h - 1))
            pieces.append(piece)
    return jnp.concatenate(pieces, axis=0)


_NT = (((1,), (1,)), ((), ()))


def _hgrn_chunk(q_raw, f_raw, v, lb, tri_ref, mask_ref, sign_ref, st_ref):
    c, w = CHUNK, q_raw.shape[1]
    heads = [slice(i * HGRN_DK, (i + 1) * HGRN_DK) for i in range(w // HGRN_DK)]
    q, k, fg = _gates(q_raw, f_raw, lb)
    b2 = jnp.dot(tri_ref[...], _split2(jnp.log(fg)), preferred_element_type=F32)
    b = b2[:, :w] + b2[:, w:]
    q16, k16, v16 = q.astype(BF16), k.astype(BF16), v.astype(BF16)
    a = [jnp.zeros((c, c), BF16) for _ in heads]
    for li, h in enumerate(LEVELS):
        if h == 1:
            e = jnp.where(sign_ref[li] > 0.0, fg, 1.0).astype(BF16)
        else:
            e = jnp.exp2((b - _boundary_rows(b, h)) * sign_ref[li]).astype(BF16)
        qe, ke = q16 * e, k16 * e
        for i, cs in enumerate(heads):
            ah = lax.dot_general(qe[:, cs], ke[:, cs], _NT, preferred_element_type=F32)
            a[i] = a[i] + ah.astype(BF16) * mask_ref[li]
    qk = q * k
    qb = (q * jnp.exp(b)).astype(BF16)
    bl = b[c - 1:c, :]
    kd = (k * jnp.exp(bl - b)).astype(BF16)
    el = jnp.exp(bl)
    outs = []
    for i, cs in enumerate(heads):
        st = st_ref[i]
        o = jnp.dot(a[i], v16[:, cs], preferred_element_type=F32)
        o = o + jnp.sum(qk[:, cs], axis=1, keepdims=True) * v[:, cs]
        o = o + lax.dot_general(qb[:, cs], st.astype(BF16), _NT, preferred_element_type=F32)
        st_ref[i] = st * el[:, cs] + jnp.dot(v[:, cs].T.astype(BF16), kd[:, cs],
                                             preferred_element_type=F32)
        outs.append(o)
    return outs


def _hgrn_prompt_kernel(q_ref, f_ref, v_ref, g_ref, lb_ref, gh_ref, tri_ref, mask_ref, sign_ref,
                        o_ref, s_ref, st_ref, *, layer, n_chunks):
    st_ref[...] = jnp.zeros_like(st_ref)
    lb = _lower_bound(lb_ref, layer)
    gh = gh_ref[layer:layer + 1, :]

    def body(ci, carry):
        rows = pl.ds(pl.multiple_of(ci * CHUNK, CHUNK), CHUNK)
        outs = _hgrn_chunk(q_ref[rows, :], f_ref[rows, :], v_ref[rows, :], lb, tri_ref, mask_ref,
                           sign_ref, st_ref)
        o = jnp.concatenate([_rms(oh, gh) for oh in outs], axis=1)
        o_ref[rows, :] = (o * _silu(g_ref[rows, :])).astype(o_ref.dtype)
        return carry

    lax.fori_loop(0, n_chunks, body, 0, unroll=2)
    for hh in range(HEADS_PER_STEP):
        s_ref[hh] = st_ref[hh].T


def _hgrn_prompt(proj, lb_logits, g_head, consts, layer, batch, seq, heads):
    hp = HEADS_PER_STEP
    n_hp = heads // hp
    wid = hp * HGRN_DK
    col = lambda grp: (lambda b, p: (b, grp * n_hp + p))
    depth = lb_logits.shape[0]
    tri, masks, signs = consts
    return pl.pallas_call(
        functools.partial(_hgrn_prompt_kernel, layer=layer, n_chunks=seq // CHUNK),
        grid=(batch, n_hp),
        in_specs=[pl.BlockSpec((seq, wid), col(0)),
                  pl.BlockSpec((seq, wid), col(1)),
                  pl.BlockSpec((seq, wid), col(2)),
                  pl.BlockSpec((seq, wid), col(3)),
                  pl.BlockSpec((depth, wid), lambda b, p: (0, p)),
                  pl.BlockSpec((depth, HGRN_DV), lambda b, p: (0, 0)),
                  pl.BlockSpec(tri.shape, lambda b, p: (0, 0)),
                  pl.BlockSpec(masks.shape, lambda b, p: (0, 0, 0)),
                  pl.BlockSpec(signs.shape, lambda b, p: (0, 0, 0))],
        out_specs=[pl.BlockSpec((seq, wid), lambda b, p: (b, p)),
                   pl.BlockSpec((None, hp, HGRN_DK, HGRN_DV), lambda b, p: (b, p, 0, 0))],
        out_shape=[jax.ShapeDtypeStruct((batch * seq, heads * HGRN_DV), BF16),
                   jax.ShapeDtypeStruct((batch, heads, HGRN_DK, HGRN_DV), F32)],
        scratch_shapes=[pltpu.VMEM((hp, HGRN_DV, HGRN_DK), F32)],
        compiler_params=_cparams(2),
        name="hgrn_prompt",
    )(proj, proj, proj, proj, lb_logits, g_head, tri, masks, signs)


def _hgrn_sample_kernel(q_ref, f_ref, v_ref, g_ref, lb_ref, gh_ref, s_ref, *rest, layer, dec_batch):
    o_ref, sn_ref, orow_ref = rest[-3:]
    phase = pl.program_id(0)

    @pl.when(phase == 0)
    def _():
        lb = _lower_bound(lb_ref, layer)
        q, k, fg = _gates(q_ref[...], f_ref[...], lb)
        v = v_ref[...]
        kt, ft = k.T, fg.T
        qf = (q * fg).astype(BF16)
        for bi in range(dec_batch):
            s_old = s_ref[bi]
            sn_ref[bi] = ft[:, bi:bi + 1] * s_old + kt[:, bi:bi + 1] * v[bi:bi + 1, :]
            orow_ref[bi:bi + 1, :] = jnp.dot(qf[bi:bi + 1, :], s_old.astype(BF16),
                                             preferred_element_type=F32)
        o = orow_ref[...] + jnp.sum(q * k, axis=1, keepdims=True) * v
        gh = gh_ref[layer:layer + 1, :]
        o_ref[...] = (_rms(o, gh) * _silu(g_ref[...])).astype(o_ref.dtype)

    @pl.when(phase > 0)
    def _():
        sn_ref[...] = jnp.zeros_like(sn_ref)


def _hgrn_sample(proj, state, lb_logits, g_head, layer, dec_batch, heads, new_state=None):
    depth = lb_logits.shape[0]
    n_phase = depth if new_state is None else 1
    head = lambda p, h: jnp.where(p == 0, h, heads - 1)
    col = lambda grp: (lambda p, h: (0, grp * heads + head(p, h)))
    in_specs = [pl.BlockSpec((dec_batch, HGRN_DK), col(0)),
                pl.BlockSpec((dec_batch, HGRN_DK), col(1)),
                pl.BlockSpec((dec_batch, HGRN_DV), col(2)),
                pl.BlockSpec((dec_batch, HGRN_DV), col(3)),
                pl.BlockSpec((depth, HGRN_DK), lambda p, h: (0, head(p, h))),
                pl.BlockSpec((depth, HGRN_DV), lambda p, h: (0, 0)),
                pl.BlockSpec((None, dec_batch, None, HGRN_DK, HGRN_DV),
                             lambda p, h: (layer, 0, head(p, h), 0, 0))]
    operands = [proj, proj, proj, proj, lb_logits, g_head, state]
    aliases = {}
    if new_state is not None:
        in_specs.append(pl.BlockSpec(memory_space=pl.ANY))
        operands.append(new_state)
        aliases = {len(operands) - 1: 1}
    return pl.pallas_call(
        functools.partial(_hgrn_sample_kernel, layer=layer, dec_batch=dec_batch),
        grid=(n_phase, heads),
        in_specs=in_specs,
        out_specs=[pl.BlockSpec((dec_batch, HGRN_DV), lambda p, h: (0, head(p, h))),
                   pl.BlockSpec((None, dec_batch, None, HGRN_DK, HGRN_DV),
                                lambda p, h: (lax.rem(layer + p, depth), 0, h, 0, 0))],
        out_shape=[jax.ShapeDtypeStruct((dec_batch, heads * HGRN_DV), BF16),
                   jax.ShapeDtypeStruct(state.shape, F32)],
        scratch_shapes=[pltpu.VMEM((dec_batch, HGRN_DV), F32)],
        input_output_aliases=aliases,
        compiler_params=_cparams(2),
        name="hgrn_sample",
    )(*operands)


POOL_CARRY = 16


def _pool_mix(pooled, wp_ref, ps_ref, o_ref, d_pg):
    for gi in range(len(POOL_WINDOWS)):
        cs = slice(gi * d_pg, (gi + 1) * d_pg)
        pm = jnp.dot(pooled[gi].astype(BF16), wp_ref[gi].astype(BF16), preferred_element_type=F32)
        o_ref[:, cs] = (pm * ps_ref[:, cs]).astype(o_ref.dtype)


def _pool_prompt_kernel(u_ref, wp_ref, ps_ref, o_ref, ext_ref, *, tt, d_pg):
    t = pl.program_id(1)

    @pl.when(t == 0)
    def _():
        ext_ref[0:POOL_CARRY, :] = jnp.zeros((POOL_CARRY, ext_ref.shape[1]), F32)

    @pl.when(t > 0)
    def _():
        ext_ref[0:POOL_CARRY, :] = ext_ref[tt:tt + POOL_CARRY, :]

    ext_ref[POOL_CARRY:POOL_CARRY + tt, :] = u_ref[...]
    pos = t * tt + lax.broadcasted_iota(jnp.int32, (tt, 1), 0)
    pooled = []
    for gi, w in enumerate(POOL_WINDOWS):
        cs = slice(gi * d_pg, (gi + 1) * d_pg)
        s = ext_ref[:, cs]
        span = 1
        while span < w:
            s = s + pltpu.roll(s, span, 0)
            span *= 2
        inv_cnt = 1.0 / jnp.minimum(pos + 1, w).astype(F32)
        pooled.append(s[POOL_CARRY:, :] * inv_cnt - u_ref[:, cs])
    _pool_mix(pooled, wp_ref, ps_ref, o_ref, d_pg)


def _pool_prompt(proj, w_pool, pool_scale, layer, batch, seq, u_col_blk, tt):
    d_pool = pool_scale.shape[1]
    n_pool, d_pg = w_pool.shape[1], w_pool.shape[2]
    nt = seq // tt
    return pl.pallas_call(
        functools.partial(_pool_prompt_kernel, tt=tt, d_pg=d_pg),
        grid=(batch, nt),
        in_specs=[pl.BlockSpec((tt, d_pool), lambda b, t: (b * nt + t, u_col_blk)),
                  pl.BlockSpec((None, n_pool, d_pg, d_pg), lambda b, t: (layer, 0, 0, 0)),
                  pl.BlockSpec((None, 1, d_pool), lambda b, t: (layer, 0, 0))],
        out_specs=pl.BlockSpec((tt, d_pool), lambda b, t: (b * nt + t, 0)),
        out_shape=jax.ShapeDtypeStruct((batch * seq, d_pool), BF16),
        scratch_shapes=[pltpu.VMEM((tt + POOL_CARRY, d_pool), F32)],
        compiler_params=_cparams(2),
        name="pool_prompt",
    )(proj, w_pool, pool_scale.reshape(-1, 1, d_pool))


def _pool_sample_kernel(u_ref, buf_ref, wp_ref, ps_ref, o_ref, *, d_pg, n_buf):
    pooled = []
    for gi, w in enumerate(POOL_WINDOWS):
        cs = slice(gi * d_pg, (gi + 1) * d_pg)
        u = u_ref[:, cs]
        s = u
        for j in range(1, w):
            s = s + buf_ref[n_buf - j, :, cs]
        cnt = float(min(PAST_LEN + 1, w))
        pooled.append(s / cnt - u)
    _pool_mix(pooled, wp_ref, ps_ref, o_ref, d_pg)


def _pool_sample(proj, buf_t, w_pool, pool_scale, layer, u_col_blk, dec_batch):
    d_pool = pool_scale.shape[1]
    n_pool, d_pg = w_pool.shape[1], w_pool.shape[2]
    n_buf = buf_t.shape[0]
    return pl.pallas_call(
        functools.partial(_pool_sample_kernel, d_pg=d_pg, n_buf=n_buf),
        grid=(1,),
        in_specs=[pl.BlockSpec((dec_batch, d_pool), lambda i: (0, u_col_blk)),
                  pl.BlockSpec(buf_t.shape, lambda i: (0, 0, 0)),
                  pl.BlockSpec((None, n_pool, d_pg, d_pg), lambda i: (layer, 0, 0, 0)),
                  pl.BlockSpec((None, 1, d_pool), lambda i: (layer, 0, 0))],
        out_specs=pl.BlockSpec((dec_batch, d_pool), lambda i: (0, 0)),
        out_shape=jax.ShapeDtypeStruct((dec_batch, d_pool), BF16),
        compiler_params=_cparams(1),
        name="pool_sample",
    )(proj, buf_t, w_pool, pool_scale.reshape(-1, 1, d_pool))


def kernel(x_prompt, x_sample, state_hgrn, state_pool, p_prompt, p_sample, g_mix, w_in, lb_logits,
           g_head, w_pool, pool_scale, w_out, g_ffn, w_gate_up, w_down, w_ple, g_ple, w_ple_gate,
           g_final):
    batch, seq, d_model = x_prompt.shape
    dec_batch = x_sample.shape[0]
    depth = w_in.shape[0]
    heads = lb_logits.shape[1] // HGRN_DK
    d_hgrn = heads * HGRN_DV
    d_pool = pool_scale.shape[1]
    d_ff = w_down.shape[1]
    pool_buf = state_pool.shape[2]
    mp = batch * seq
    assert x_sample.shape[1] == 1 and seq % CHUNK == 0 and heads % HEADS_PER_STEP == 0
    assert dec_batch % ROW_SUBTILE == 0
    assert w_in.shape[2] == 2 * heads * HGRN_DK + 2 * d_hgrn + d_pool

    bm = 1024
    bm_row = 512
    u_col_blk = (2 * heads * HGRN_DK + 2 * d_hgrn) // d_pool
    consts = _chunk_constants()

    h_p = x_prompt.reshape(mp, d_model)
    h_s = x_sample.reshape(dec_batch, d_model)
    pe_p = p_prompt.reshape(depth, mp, -1)
    pe_s = p_sample.reshape(depth, dec_batch, -1)

    s_prompt, pool_prompt, pool_sample = [], [], []
    new_state = None
    n_p, n_s = h_p, h_s
    for l in range(depth):
        proj_p, proj_s = _matmul([n_p], [n_s], w_in, l, n_cols=w_in.shape[2], bm=bm, bn=1024,
                                 norm_gain=g_mix[0].reshape(1, d_model) if l == 0 else None,
                                 name="proj_in")

        o_p, st_p = _hgrn_prompt(proj_p, lb_logits, g_head, consts, l, batch, seq, heads)
        o_s, new_state = _hgrn_sample(proj_s, state_hgrn, lb_logits, g_head, l, dec_batch, heads,
                                      new_state)
        pm_p = _pool_prompt(proj_p, w_pool, pool_scale, l, batch, seq, u_col_blk, tt=512)
        buf_t = jnp.transpose(state_pool[l], (1, 0, 2))
        pm_s = _pool_sample(proj_s, buf_t, w_pool, pool_scale, l, u_col_blk, dec_batch)
        s_prompt.append(st_p)
        u_off = u_col_blk * d_pool
        pool_prompt.append(proj_p.reshape(batch, seq, -1)[:, seq - pool_buf:, u_off:])
        pool_sample.append(jnp.concatenate([state_pool[l][:, 1:], proj_s[:, None, u_off:]], axis=1))

        h_p, n_p, h_s, n_s = _proj_out(o_p, pm_p, h_p, o_s, pm_s, h_s, w_out, g_ffn, l, bm_row)
        act_p, act_s = _matmul([n_p], [n_s], w_gate_up, l, n_cols=d_ff, bm=bm, bn=512,
                               mode="swiglu", up_offset=d_ff, out_dtypes=(BF16,), name="ffn_up")
        h_p, h_s = _matmul([act_p], [act_s], w_down, l, n_cols=d_model, bm=bm // 2, bn=512,
                           mode="res", extras_p=(h_p,), extras_s=(h_s,), name="ffn_down")
        if l + 1 < depth:
            h_p, n_p, h_s, n_s = _ple(h_p, pe_p, h_s, pe_s, w_ple_gate, w_ple, g_ple,
                                      g_mix[l + 1].reshape(1, d_model), l, bm_row,
                                      write_h=True, norm_dtype=BF16)
        else:
            y_p, y_s = _ple(h_p, pe_p, h_s, pe_s, w_ple_gate, w_ple, g_ple,
                            g_final.reshape(1, d_model), l, bm_row, write_h=False, norm_dtype=F32)

    return (y_p.reshape(batch, seq, d_model), y_s.reshape(dec_batch, 1, d_model),
            jnp.stack(s_prompt), jnp.stack(pool_prompt), new_state, jnp.stack(pool_sample))
```

```python
import functools

import numpy as np
import jax
import jax.numpy as jnp
from jax import lax
from jax.experimental import pallas as pl
from jax.experimental.pallas import tpu as pltpu

F32 = jnp.float32
BF16 = jnp.bfloat16

EPS = 1e-6
LOG2E = 1.4426950408889634
HGRN_DK = 128
HGRN_DV = 128
POOL_WINDOWS = (2, 4, 8, 16)
PAST_LEN = 16384
SUBLANES = 8
CHUNK = 128
LEVELS = (64, 32, 16, 8, 4, 2, 1)
HEADS_PER_STEP = 4
ROW_SUBTILE = 128

V7X_VMEM_BYTES = 64 * 1024 * 1024
VMEM_LIMIT_BYTES = V7X_VMEM_BYTES - 8 * 1024 * 1024


def _cparams(n_axes):
    return pltpu.CompilerParams(
        dimension_semantics=("arbitrary",) * n_axes, vmem_limit_bytes=VMEM_LIMIT_BYTES)


def _silu(x):
    hx = 0.5 * x
    return hx + hx * jnp.tanh(hx)


def _rms(x, g):
    return x * lax.rsqrt(jnp.mean(x * x, axis=-1, keepdims=True) + EPS) * g


_N_EXTRA = {"plain": 0, "res": 1, "swiglu": 0}


def _mm_kernel(*refs, n_x, n_w, n_out, mode, norm_x):
    n_e = _N_EXTRA[mode]
    pos = 0

    def take(n):
        nonlocal pos
        out = refs[pos:pos + n]
        pos += n
        return out

    xp, xs, w_refs = take(n_x), take(n_x), take(n_w)
    ep, es, gain = take(n_e), take(n_e), take(1 if norm_x else 0)
    op, osm, wbf = take(n_out), take(n_out), take(n_w)
    i = pl.program_id(1)

    @pl.when(i == 0)
    def _():
        for w_ref, wb in zip(w_refs, wbf):
            wb[...] = w_ref[...].astype(BF16)

    def compute(x_refs, extra, outs):
        if norm_x:
            cast = [_rms(x_refs[0][...], gain[0][...]).astype(BF16)]
        else:
            cast = [xr[...].astype(BF16) for xr in x_refs]
        xb = cast[0] if n_x == 1 else jnp.concatenate(cast, axis=1)
        acc = jnp.dot(xb, wbf[0][...], preferred_element_type=F32)
        if mode == "plain":
            res = acc
        elif mode == "res":
            res = extra[0][...] + acc
        else:
            res = _silu(acc) * jnp.dot(xb, wbf[1][...], preferred_element_type=F32)
        for o_ref in outs:
            o_ref[...] = res.astype(o_ref.dtype)

    @pl.when(i == 0)
    def _():
        compute(xs, es, osm)

    @pl.when(i > 0)
    def _():
        compute(xp, ep, op)


def _matmul(xs_p, xs_s, w, layer, *, n_cols, bm, bn, mode="plain", extras_p=(), extras_s=(),
            out_dtypes=(F32,), up_offset=None, norm_gain=None, single_buffer_w=False, name="mm"):
    mp, ms = xs_p[0].shape[0], xs_s[0].shape[0]
    k = w.shape[1]
    assert sum(x.shape[1] for x in xs_p) == k and mp % bm == 0 and n_cols % bn == 0
    npt = mp // bm
    grid = (n_cols // bn, npt + 1)
    prow = lambda j, i: (jnp.maximum(i - 1, 0), 0)
    ptile = lambda j, i: (jnp.maximum(i - 1, 0), j)
    in_specs = [pl.BlockSpec((bm, x.shape[1]), prow) for x in xs_p]
    in_specs += [pl.BlockSpec((ms, x.shape[1]), lambda j, i: (0, 0)) for x in xs_s]
    in_specs.append(pl.BlockSpec((None, k, bn), lambda j, i: (layer, 0, j),
                                 pipeline_mode=pl.Buffered(1) if single_buffer_w else None))
    operands = list(xs_p) + list(xs_s) + [w]
    n_w = 1
    if mode == "swiglu":
        off = up_offset // bn
        in_specs.append(pl.BlockSpec((None, k, bn), lambda j, i: (layer, 0, j + off)))
        operands.append(w)
        n_w = 2
    in_specs += [pl.BlockSpec((bm, bn), ptile) for _ in extras_p]
    in_specs += [pl.BlockSpec((ms, bn), lambda j, i: (0, j)) for _ in extras_s]
    operands += list(extras_p) + list(extras_s)
    if norm_gain is not None:
        assert len(xs_p) == 1
        in_specs.append(pl.BlockSpec((1, k), lambda j, i: (0, 0)))
        operands.append(norm_gain)
    out_specs = ([pl.BlockSpec((bm, bn), ptile) for _ in out_dtypes]
                 + [pl.BlockSpec((ms, bn), lambda j, i: (0, j)) for _ in out_dtypes])
    out_shape = ([jax.ShapeDtypeStruct((mp, n_cols), dt) for dt in out_dtypes]
                 + [jax.ShapeDtypeStruct((ms, n_cols), dt) for dt in out_dtypes])
    return pl.pallas_call(
        functools.partial(_mm_kernel, n_x=len(xs_p), n_w=n_w, n_out=len(out_dtypes), mode=mode,
                          norm_x=norm_gain is not None),
        grid=grid,
        in_specs=in_specs,
        out_specs=out_specs,
        out_shape=out_shape,
        scratch_shapes=[pltpu.VMEM((k, bn), BF16) for _ in range(n_w)],
        compiler_params=_cparams(2),
        name=name,
    )(*operands)


def _resident(shape, index_map):
    return pl.BlockSpec(shape, index_map, pipeline_mode=pl.Buffered(1))


def _proj_out_kernel(op_ref, pmp_ref, hp_ref, os_ref, pms_ref, hs_ref, w_ref, g_ref,
                     hp_out, np_out, hs_out, ns_out, wbf):
    i = pl.program_id(0)

    def compute(o_ref, pm_ref, h_ref, h_out, n_out):
        for r0 in range(0, h_ref.shape[0], ROW_SUBTILE):
            rows = slice(r0, r0 + ROW_SUBTILE)
            xb = jnp.concatenate([o_ref[rows, :], pm_ref[rows, :]], axis=1)
            hn = h_ref[rows, :] + jnp.dot(xb, wbf[...], preferred_element_type=F32)
            h_out[rows, :] = hn
            n_out[rows, :] = _rms(hn, g_ref[...]).astype(n_out.dtype)

    @pl.when(i == 0)
    def _():
        wbf[...] = w_ref[...].astype(BF16)
        compute(os_ref, pms_ref, hs_ref, hs_out, ns_out)

    @pl.when(i > 0)
    def _():
        compute(op_ref, pmp_ref, hp_ref, hp_out, np_out)


def _proj_out(o_p, pm_p, h_p, o_s, pm_s, h_s, w_out, g_ffn, layer, bm):
    mp, d = h_p.shape
    ms = h_s.shape[0]
    ko, kp = o_p.shape[1], pm_p.shape[1]
    prow = lambda i: (jnp.maximum(i - 1, 0), 0)
    srow = lambda i: (0, 0)
    return pl.pallas_call(
        _proj_out_kernel,
        grid=(mp // bm + 1,),
        in_specs=[pl.BlockSpec((bm, ko), prow), pl.BlockSpec((bm, kp), prow), pl.BlockSpec((bm, d), prow),
                  pl.BlockSpec((ms, ko), srow), pl.BlockSpec((ms, kp), srow), pl.BlockSpec((ms, d), srow),
                  _resident((None, ko + kp, d), lambda i: (layer, 0, 0)),
                  pl.BlockSpec((None, 1, d), lambda i: (layer, 0, 0))],
        out_specs=[pl.BlockSpec((bm, d), prow), pl.BlockSpec((bm, d), prow),
                   pl.BlockSpec((ms, d), srow), pl.BlockSpec((ms, d), srow)],
        out_shape=[jax.ShapeDtypeStruct((mp, d), F32), jax.ShapeDtypeStruct((mp, d), BF16),
                   jax.ShapeDtypeStruct((ms, d), F32), jax.ShapeDtypeStruct((ms, d), BF16)],
        scratch_shapes=[pltpu.VMEM((ko + kp, d), BF16)],
        compiler_params=_cparams(1),
        name="proj_out",
    )(o_p, pm_p, h_p, o_s, pm_s, h_s, w_out, g_ffn.reshape(-1, 1, d))


def _ple_kernel(*refs, write_h):
    hp_ref, pp_ref, hs_ref, ps_ref, wg_ref, wp_ref, ge_ref, gn_ref = refs[:8]
    outs = refs[8:-2]
    wgb, wpb = refs[-2:]
    if write_h:
        hp_out, np_out, hs_out, ns_out = outs
    else:
        (np_out, ns_out), hp_out, hs_out = outs, None, None
    i = pl.program_id(0)

    def compute(h_ref, p_ref, h_out, n_out):
        for r0 in range(0, h_ref.shape[0], ROW_SUBTILE):
            rows = slice(r0, r0 + ROW_SUBTILE)
            hv = h_ref[rows, :]
            gate = jax.nn.sigmoid(jnp.dot(hv.astype(BF16), wgb[...], preferred_element_type=F32))
            e = _rms(jnp.dot(p_ref[rows, :].astype(BF16), wpb[...], preferred_element_type=F32),
                     ge_ref[...])
            hn = hv + gate * e
            if write_h:
                h_out[rows, :] = hn
            n_out[rows, :] = _rms(hn, gn_ref[...]).astype(n_out.dtype)

    @pl.when(i == 0)
    def _():
        wgb[...] = wg_ref[...].astype(BF16)
        wpb[...] = wp_ref[...].astype(BF16)
        compute(hs_ref, ps_ref, hs_out, ns_out)

    @pl.when(i > 0)
    def _():
        compute(hp_ref, pp_ref, hp_out, np_out)


def _ple(h_p, pe_p, h_s, pe_s, w_gate, w_ple, g_ple, g_next, layer, bm, *, write_h, norm_dtype):
    mp, d = h_p.shape
    ms = h_s.shape[0]
    kp = pe_p.shape[2]
    prow = lambda i: (jnp.maximum(i - 1, 0), 0)
    srow = lambda i: (0, 0)
    n_specs = [pl.BlockSpec((bm, d), prow), pl.BlockSpec((ms, d), srow)]
    n_shapes = [jax.ShapeDtypeStruct((mp, d), norm_dtype), jax.ShapeDtypeStruct((ms, d), norm_dtype)]
    if write_h:
        out_specs = [n_specs[0], n_specs[0], n_specs[1], n_specs[1]]
        out_shape = [jax.ShapeDtypeStruct((mp, d), F32), n_shapes[0],
                     jax.ShapeDtypeStruct((ms, d), F32), n_shapes[1]]
    else:
        out_specs, out_shape = n_specs, n_shapes
    return pl.pallas_call(
        functools.partial(_ple_kernel, write_h=write_h),
        grid=(mp // bm + 1,),
        in_specs=[pl.BlockSpec((bm, d), prow),
                  pl.BlockSpec((None, bm, kp), lambda i: (layer, jnp.maximum(i - 1, 0), 0)),
                  pl.BlockSpec((ms, d), srow),
                  pl.BlockSpec((None, ms, kp), lambda i: (layer, 0, 0)),
                  _resident((None, d, d), lambda i: (layer, 0, 0)),
                  _resident((None, kp, d), lambda i: (layer, 0, 0)),
                  pl.BlockSpec((None, 1, d), lambda i: (layer, 0, 0)),
                  pl.BlockSpec((1, d), lambda i: (0, 0))],
        out_specs=out_specs,
        out_shape=out_shape,
        scratch_shapes=[pltpu.VMEM((d, d), BF16), pltpu.VMEM((kp, d), BF16)],
        compiler_params=_cparams(1),
        name="ple",
    )(h_p, pe_p, h_s, pe_s, w_gate, w_ple, g_ple.reshape(-1, 1, d), g_next)


def _chunk_constants():
    c = CHUNK
    t = np.arange(c)[:, None]
    s = np.arange(c)[None, :]
    masks, signs = [], []
    for h in LEVELS:
        same = (t // (2 * h)) == (s // (2 * h))
        upper = (t % (2 * h)) >= h
        masks.append((same & upper & ((s % (2 * h)) < h)).astype(np.float32))
        signs.append(np.broadcast_to(np.where(upper, LOG2E, -LOG2E),
                                     (c, HEADS_PER_STEP * HGRN_DK)).astype(np.float32))
    tri = (s <= t).astype(np.float32)
    return (jnp.asarray(tri, dtype=BF16), jnp.asarray(np.stack(masks), dtype=BF16),
            jnp.asarray(np.stack(signs), dtype=F32))


def _lower_bound(lb_ref, layer):
    lg = lb_ref[...]
    e = jnp.exp(lg - jnp.max(lg, axis=0, keepdims=True))
    s = e / jnp.sum(e, axis=0, keepdims=True)
    c = s[0:1, :]
    for r in range(1, layer + 1):
        c = c + s[r:r + 1, :]
    return c - s[0:1, :]


def _gates(q_raw, f_raw, lb):
    q = _silu(q_raw) * (HGRN_DK ** -0.5)
    fg = lb + (1.0 - lb) * jax.nn.sigmoid(f_raw)
    return q, 1.0 - fg, fg


def _split2(x):
    hi = x.astype(BF16)
    lo = (x - hi.astype(F32)).astype(BF16)
    return jnp.concatenate([hi, lo], axis=1)


def _boundary_rows(b, h):
    n_grp = CHUNK // SUBLANES

    def bcast(grp, sub):
        r = grp * SUBLANES + sub
        return jnp.broadcast_to(b[r:r + 1, :], (SUBLANES, b.shape[1]))

    sub_id = lax.broadcasted_iota(jnp.int32, (SUBLANES, b.shape[1]), 0)
    pieces = []
    for grp in range(n_grp):
        if h >= SUBLANES:
            per_blk = 2 * h // SUBLANES
            pieces.append(bcast((grp // per_blk) * per_blk + per_blk // 2 - 1, SUBLANES - 1))
        else:
            piece = bcast(grp, h - 1)
            for blk in range(1, SUBLANES // (2 * h)):
                piece = jnp.where(sub_id < blk * 2 * h, piece, bcast(grp, blk * 2 * h + h - 1))
            pieces.append(piece)
    return jnp.concatenate(pieces, axis=0)


_NT = (((1,), (1,)), ((), ()))


def _hgrn_chunk(q_raw, f_raw, v, lb, tri_ref, mask_ref, sign_ref, st_ref):
    c, w = CHUNK, q_raw.shape[1]
    heads = [slice(i * HGRN_DK, (i + 1) * HGRN_DK) for i in range(w // HGRN_DK)]
    q, k, fg = _gates(q_raw, f_raw, lb)
    b2 = jnp.dot(tri_ref[...], _split2(jnp.log(fg)), preferred_element_type=F32)
    b = b2[:, :w] + b2[:, w:]
    q16, k16, v16 = q.astype(BF16), k.astype(BF16), v.astype(BF16)
    a = [jnp.zeros((c, c), BF16) for _ in heads]
    for li, h in enumerate(LEVELS):
        if h == 1:
            e = jnp.where(sign_ref[li] > 0.0, fg, 1.0).astype(BF16)
        else:
            e = jnp.exp2((b - _boundary_rows(b, h)) * sign_ref[li]).astype(BF16)
        qe, ke = q16 * e, k16 * e
        for i, cs in enumerate(heads):
            ah = lax.dot_general(qe[:, cs], ke[:, cs], _NT, preferred_element_type=F32)
            a[i] = a[i] + ah.astype(BF16) * mask_ref[li]
    qk = q * k
    qb = (q * jnp.exp(b)).astype(BF16)
    bl = b[c - 1:c, :]
    kd = (k * jnp.exp(bl - b)).astype(BF16)
    el = jnp.exp(bl)
    outs = []
    for i, cs in enumerate(heads):
        st = st_ref[i]
        o = jnp.dot(a[i], v16[:, cs], preferred_element_type=F32)
        o = o + jnp.sum(qk[:, cs], axis=1, keepdims=True) * v[:, cs]
        o = o + lax.dot_general(qb[:, cs], st.astype(BF16), _NT, preferred_element_type=F32)
        st_ref[i] = st * el[:, cs] + jnp.dot(v[:, cs].T.astype(BF16), kd[:, cs],
                                             preferred_element_type=F32)
        outs.append(o)
    return outs


def _hgrn_prompt_kernel(q_ref, f_ref, v_ref, g_ref, lb_ref, gh_ref, tri_ref, mask_ref, sign_ref,
                        o_ref, s_ref, st_ref, *, layer, n_chunks):
    st_ref[...] = jnp.zeros_like(st_ref)
    lb = _lower_bound(lb_ref, layer)
    gh = gh_ref[layer:layer + 1, :]

    def body(ci, carry):
        rows = pl.ds(pl.multiple_of(ci * CHUNK, CHUNK), CHUNK)
        outs = _hgrn_chunk(q_ref[rows, :], f_ref[rows, :], v_ref[rows, :], lb, tri_ref, mask_ref,
                           sign_ref, st_ref)
        o = jnp.concatenate([_rms(oh, gh) for oh in outs], axis=1)
        o_ref[rows, :] = (o * _silu(g_ref[rows, :])).astype(o_ref.dtype)
        return carry

    lax.fori_loop(0, n_chunks, body, 0, unroll=2)
    for hh in range(HEADS_PER_STEP):
        s_ref[hh] = st_ref[hh].T


def _hgrn_prompt(proj, lb_logits, g_head, consts, layer, batch, seq, heads):
    hp = HEADS_PER_STEP
    n_hp = heads // hp
    wid = hp * HGRN_DK
    col = lambda grp: (lambda b, p: (b, grp * n_hp + p))
    depth = lb_logits.shape[0]
    tri, masks, signs = consts
    return pl.pallas_call(
        functools.partial(_hgrn_prompt_kernel, layer=layer, n_chunks=seq // CHUNK),
        grid=(batch, n_hp),
        in_specs=[pl.BlockSpec((seq, wid), col(0)),
                  pl.BlockSpec((seq, wid), col(1)),
                  pl.BlockSpec((seq, wid), col(2)),
                  pl.BlockSpec((seq, wid), col(3)),
                  pl.BlockSpec((depth, wid), lambda b, p: (0, p)),
                  pl.BlockSpec((depth, HGRN_DV), lambda b, p: (0, 0)),
                  pl.BlockSpec(tri.shape, lambda b, p: (0, 0)),
                  pl.BlockSpec(masks.shape, lambda b, p: (0, 0, 0)),
                  pl.BlockSpec(signs.shape, lambda b, p: (0, 0, 0))],
        out_specs=[pl.BlockSpec((seq, wid), lambda b, p: (b, p)),
                   pl.BlockSpec((None, hp, HGRN_DK, HGRN_DV), lambda b, p: (b, p, 0, 0))],
        out_shape=[jax.ShapeDtypeStruct((batch * seq, heads * HGRN_DV), BF16),
                   jax.ShapeDtypeStruct((batch, heads, HGRN_DK, HGRN_DV), F32)],
        scratch_shapes=[pltpu.VMEM((hp, HGRN_DV, HGRN_DK), F32)],
        compiler_params=_cparams(2),
        name="hgrn_prompt",
    )(proj, proj, proj, proj, lb_logits, g_head, tri, masks, signs)


def _hgrn_sample_kernel(q_ref, f_ref, v_ref, g_ref, lb_ref, gh_ref, s_ref, *rest, layer, dec_batch):
    o_ref, sn_ref, orow_ref = rest[-3:]
    phase = pl.program_id(0)

    @pl.when(phase == 0)
    def _():
        lb = _lower_bound(lb_ref, layer)
        q, k, fg = _gates(q_ref[...], f_ref[...], lb)
        v = v_ref[...]
        kt, ft = k.T, fg.T
        qf = (q * fg).astype(BF16)
        for bi in range(dec_batch):
            s_old = s_ref[bi]
            sn_ref[bi] = ft[:, bi:bi + 1] * s_old + kt[:, bi:bi + 1] * v[bi:bi + 1, :]
            orow_ref[bi:bi + 1, :] = jnp.dot(qf[bi:bi + 1, :], s_old.astype(BF16),
                                             preferred_element_type=F32)
        o = orow_ref[...] + jnp.sum(q * k, axis=1, keepdims=True) * v
        gh = gh_ref[layer:layer + 1, :]
        o_ref[...] = (_rms(o, gh) * _silu(g_ref[...])).astype(o_ref.dtype)

    @pl.when(phase > 0)
    def _():
        sn_ref[...] = jnp.zeros_like(sn_ref)


def _hgrn_sample(proj, state, lb_logits, g_head, layer, dec_batch, heads, new_state=None):
    depth = lb_logits.shape[0]
    n_phase = depth if new_state is None else 1
    head = lambda p, h: jnp.where(p == 0, h, heads - 1)
    col = lambda grp: (lambda p, h: (0, grp * heads + head(p, h)))
    in_specs = [pl.BlockSpec((dec_batch, HGRN_DK), col(0)),
                pl.BlockSpec((dec_batch, HGRN_DK), col(1)),
                pl.BlockSpec((dec_batch, HGRN_DV), col(2)),
                pl.BlockSpec((dec_batch, HGRN_DV), col(3)),
                pl.BlockSpec((depth, HGRN_DK), lambda p, h: (0, head(p, h))),
                pl.BlockSpec((depth, HGRN_DV), lambda p, h: (0, 0)),
                pl.BlockSpec((None, dec_batch, None, HGRN_DK, HGRN_DV),
                             lambda p, h: (layer, 0, head(p, h), 0, 0))]
    operands = [proj, proj, proj, proj, lb_logits, g_head, state]
    aliases = {}
    if new_state is not None:
        in_specs.append(pl.BlockSpec(memory_space=pl.ANY))
        operands.append(new_state)
        aliases = {len(operands) - 1: 1}
    return pl.pallas_call(
        functools.partial(_hgrn_sample_kernel, layer=layer, dec_batch=dec_batch),
        grid=(n_phase, heads),
        in_specs=in_specs,
        out_specs=[pl.BlockSpec((dec_batch, HGRN_DV), lambda p, h: (0, head(p, h))),
                   pl.BlockSpec((None, dec_batch, None, HGRN_DK, HGRN_DV),
                                lambda p, h: (lax.rem(layer + p, depth), 0, h, 0, 0))],
        out_shape=[jax.ShapeDtypeStruct((dec_batch, heads * HGRN_DV), BF16),
                   jax.ShapeDtypeStruct(state.shape, F32)],
        scratch_shapes=[pltpu.VMEM((dec_batch, HGRN_DV), F32)],
        input_output_aliases=aliases,
        compiler_params=_cparams(2),
        name="hgrn_sample",
    )(*operands)


POOL_CARRY = 16


def _pool_mix(pooled, wp_ref, ps_ref, o_ref, d_pg):
    for gi in range(len(POOL_WINDOWS)):
        cs = slice(gi * d_pg, (gi + 1) * d_pg)
        pm = jnp.dot(pooled[gi].astype(BF16), wp_ref[gi].astype(BF16), preferred_element_type=F32)
        o_ref[:, cs] = (pm * ps_ref[:, cs]).astype(o_ref.dtype)


def _pool_prompt_kernel(u_ref, wp_ref, ps_ref, o_ref, ext_ref, *, tt, d_pg):
    t = pl.program_id(1)

    @pl.when(t == 0)
    def _():
        ext_ref[0:POOL_CARRY, :] = jnp.zeros((POOL_CARRY, ext_ref.shape[1]), F32)

    @pl.when(t > 0)
    def _():
        ext_ref[0:POOL_CARRY, :] = ext_ref[tt:tt + POOL_CARRY, :]

    ext_ref[POOL_CARRY:POOL_CARRY + tt, :] = u_ref[...]
    pos = t * tt + lax.broadcasted_iota(jnp.int32, (tt, 1), 0)
    pooled = []
    for gi, w in enumerate(POOL_WINDOWS):
        cs = slice(gi * d_pg, (gi + 1) * d_pg)
        s = ext_ref[:, cs]
        span = 1
        while span < w:
            s = s + pltpu.roll(s, span, 0)
            span *= 2
        inv_cnt = 1.0 / jnp.minimum(pos + 1, w).astype(F32)
        pooled.append(s[POOL_CARRY:, :] * inv_cnt - u_ref[:, cs])
    _pool_mix(pooled, wp_ref, ps_ref, o_ref, d_pg)


def _pool_prompt(proj, w_pool, pool_scale, layer, batch, seq, u_col_blk, tt):
    d_pool = pool_scale.shape[1]
    n_pool, d_pg = w_pool.shape[1], w_pool.shape[2]
    nt = seq // tt
    return pl.pallas_call(
        functools.partial(_pool_prompt_kernel, tt=tt, d_pg=d_pg),
        grid=(batch, nt),
        in_specs=[pl.BlockSpec((tt, d_pool), lambda b, t: (b * nt + t, u_col_blk)),
                  pl.BlockSpec((None, n_pool, d_pg, d_pg), lambda b, t: (layer, 0, 0, 0)),
                  pl.BlockSpec((None, 1, d_pool), lambda b, t: (layer, 0, 0))],
        out_specs=pl.BlockSpec((tt, d_pool), lambda b, t: (b * nt + t, 0)),
        out_shape=jax.ShapeDtypeStruct((batch * seq, d_pool), BF16),
        scratch_shapes=[pltpu.VMEM((tt + POOL_CARRY, d_pool), F32)],
        compiler_params=_cparams(2),
        name="pool_prompt",
    )(proj, w_pool, pool_scale.reshape(-1, 1, d_pool))


def _pool_sample_kernel(u_ref, buf_ref, wp_ref, ps_ref, o_ref, *, d_pg, n_buf):
    pooled = []
    for gi, w in enumerate(POOL_WINDOWS):
        cs = slice(gi * d_pg, (gi + 1) * d_pg)
        u = u_ref[:, cs]
        s = u
        for j in range(1, w):
            s = s + buf_ref[n_buf - j, :, cs]
        cnt = float(min(PAST_LEN + 1, w))
        pooled.append(s / cnt - u)
    _pool_mix(pooled, wp_ref, ps_ref, o_ref, d_pg)


def _pool_sample(proj, buf_t, w_pool, pool_scale, layer, u_col_blk, dec_batch):
    d_pool = pool_scale.shape[1]
    n_pool, d_pg = w_pool.shape[1], w_pool.shape[2]
    n_buf = buf_t.shape[0]
    return pl.pallas_call(
        functools.partial(_pool_sample_kernel, d_pg=d_pg, n_buf=n_buf),
        grid=(1,),
        in_specs=[pl.BlockSpec((dec_batch, d_pool), lambda i: (0, u_col_blk)),
                  pl.BlockSpec(buf_t.shape, lambda i: (0, 0, 0)),
                  pl.BlockSpec((None, n_pool, d_pg, d_pg), lambda i: (layer, 0, 0, 0)),
                  pl.BlockSpec((None, 1, d_pool), lambda i: (layer, 0, 0))],
        out_specs=pl.BlockSpec((dec_batch, d_pool), lambda i: (0, 0)),
        out_shape=jax.ShapeDtypeStruct((dec_batch, d_pool), BF16),
        compiler_params=_cparams(1),
        name="pool_sample",
    )(proj, buf_t, w_pool, pool_scale.reshape(-1, 1, d_pool))


def kernel(x_prompt, x_sample, state_hgrn, state_pool, p_prompt, p_sample, g_mix, w_in, lb_logits,
           g_head, w_pool, pool_scale, w_out, g_ffn, w_gate_up, w_down, w_ple, g_ple, w_ple_gate,
           g_final):
    batch, seq, d_model = x_prompt.shape
    dec_batch = x_sample.shape[0]
    depth = w_in.shape[0]
    heads = lb_logits.shape[1] // HGRN_DK
    d_hgrn = heads * HGRN_DV
    d_pool = pool_scale.shape[1]
    d_ff = w_down.shape[1]
    pool_buf = state_pool.shape[2]
    mp = batch * seq
    assert x_sample.shape[1] == 1 and seq % CHUNK == 0 and heads % HEADS_PER_STEP == 0
    assert dec_batch % ROW_SUBTILE == 0
    assert w_in.shape[2] == 2 * heads * HGRN_DK + 2 * d_hgrn + d_pool

    bm = 1024
    bm_row = 512
    u_col_blk = (2 * heads * HGRN_DK + 2 * d_hgrn) // d_pool
    consts = _chunk_constants()

    h_p = x_prompt.reshape(mp, d_model)
    h_s = x_sample.reshape(dec_batch, d_model)
    pe_p = p_prompt.reshape(depth, mp, -1)
    pe_s = p_sample.reshape(depth, dec_batch, -1)

    s_prompt, pool_prompt, pool_sample = [], [], []
    new_state = None
    n_p, n_s = h_p, h_s
    for l in range(depth):
        proj_p, proj_s = _matmul([n_p], [n_s], w_in, l, n_cols=w_in.shape[2], bm=bm, bn=1024,
                                 norm_gain=g_mix[0].reshape(1, d_model) if l == 0 else None,
                                 name="proj_in")

        o_p, st_p = _hgrn_prompt(proj_p, lb_logits, g_head, consts, l, batch, seq, heads)
        o_s, new_state = _hgrn_sample(proj_s, state_hgrn, lb_logits, g_head, l, dec_batch, heads,
                                      new_state)
        pm_p = _pool_prompt(proj_p, w_pool, pool_scale, l, batch, seq, u_col_blk, tt=512)
        buf_t = jnp.transpose(state_pool[l], (1, 0, 2))
        pm_s = _pool_sample(proj_s, buf_t, w_pool, pool_scale, l, u_col_blk, dec_batch)
        s_prompt.append(st_p)
        u_off = u_col_blk * d_pool
        pool_prompt.append(proj_p.reshape(batch, seq, -1)[:, seq - pool_buf:, u_off:])
        pool_sample.append(jnp.concatenate([state_pool[l][:, 1:], proj_s[:, None, u_off:]], axis=1))

        h_p, n_p, h_s, n_s = _proj_out(o_p, pm_p, h_p, o_s, pm_s, h_s, w_out, g_ffn, l, bm_row)
        act_p, act_s = _matmul([n_p], [n_s], w_gate_up, l, n_cols=d_ff, bm=bm * (1 + l), bn=512,
                               mode="swiglu", up_offset=d_ff, out_dtypes=(BF16,), name="ffn_up")
        if l == 0:
            h_p, h_s = _matmul([act_p], [act_s], w_down, l, n_cols=d_model, bm=bm // 2, bn=512,
                               mode="res", extras_p=(h_p,), extras_s=(h_s,), name="ffn_down")
        else:
            h_p, h_s = _matmul([act_p], [act_s], w_down, l, n_cols=d_model, bm=bm // 4, bn=1024,
                               mode="res", extras_p=(h_p,), extras_s=(h_s,), single_buffer_w=True,
                               name="ffn_down")
        if l + 1 < depth:
            h_p, n_p, h_s, n_s = _ple(h_p, pe_p, h_s, pe_s, w_ple_gate, w_ple, g_ple,
                                      g_mix[l + 1].reshape(1, d_model), l, bm_row,
                                      write_h=True, norm_dtype=BF16)
        else:
            y_p, y_s = _ple(h_p, pe_p, h_s, pe_s, w_ple_gate, w_ple, g_ple,
                            g_final.reshape(1, d_model), l, bm_row, write_h=False, norm_dtype=F32)

    return (y_p.reshape(batch, seq, d_model), y_s.reshape(dec_batch, 1, d_model),
            jnp.stack(s_prompt), jnp.stack(pool_prompt), new_state, jnp.stack(pool_sample))
```

```python
import functools

import numpy as np
import jax
import jax.numpy as jnp
from jax import lax
from jax.experimental import pallas as pl
from jax.experimental.pallas import tpu as pltpu

F32 = jnp.float32
BF16 = jnp.bfloat16

EPS = 1e-6
LOG2E = 1.4426950408889634
HGRN_DK = 128
HGRN_DV = 128
POOL_WINDOWS = (2, 4, 8, 16)
PAST_LEN = 16384
SUBLANES = 8
CHUNK = 128
LEVELS = (64, 32, 16, 8, 4, 2, 1)
HEADS_PER_STEP = 4
SEQ_TILE = 1024
ROW_SUBTILE = 128

V7X_VMEM_BYTES = 64 * 1024 * 1024
VMEM_LIMIT_BYTES = V7X_VMEM_BYTES - 8 * 1024 * 1024


def _cparams(n_axes):
    return pltpu.CompilerParams(
        dimension_semantics=("arbitrary",) * n_axes, vmem_limit_bytes=VMEM_LIMIT_BYTES)


def _silu(x):
    hx = 0.5 * x
    return hx + hx * jnp.tanh(hx)


def _rms(x, g):
    return x * lax.rsqrt(jnp.mean(x * x, axis=-1, keepdims=True) + EPS) * g


_N_EXTRA = {"plain": 0, "res": 1, "swiglu": 0}


def _mm_kernel(*refs, n_x, n_w, n_out, mode, norm_x):
    n_e = _N_EXTRA[mode]
    pos = 0

    def take(n):
        nonlocal pos
        out = refs[pos:pos + n]
        pos += n
        return out

    xp, xs, w_refs = take(n_x), take(n_x), take(n_w)
    ep, es, gain = take(n_e), take(n_e), take(1 if norm_x else 0)
    op, osm, wbf = take(n_out), take(n_out), take(n_w)
    i = pl.program_id(1)

    @pl.when(i == 0)
    def _():
        for w_ref, wb in zip(w_refs, wbf):
            wb[...] = w_ref[...].astype(BF16)

    def compute(x_refs, extra, outs):
        if norm_x:
            cast = [_rms(x_refs[0][...], gain[0][...]).astype(BF16)]
        else:
            cast = [xr[...].astype(BF16) for xr in x_refs]
        xb = cast[0] if n_x == 1 else jnp.concatenate(cast, axis=1)
        acc = jnp.dot(xb, wbf[0][...], preferred_element_type=F32)
        if mode == "plain":
            res = acc
        elif mode == "res":
            res = extra[0][...] + acc
        else:
            res = _silu(acc) * jnp.dot(xb, wbf[1][...], preferred_element_type=F32)
        for o_ref in outs:
            o_ref[...] = res.astype(o_ref.dtype)

    @pl.when(i == 0)
    def _():
        compute(xs, es, osm)

    @pl.when(i > 0)
    def _():
        compute(xp, ep, op)


def _matmul(xs_p, xs_s, w, layer, *, n_cols, bm, bn, mode="plain", extras_p=(), extras_s=(),
            out_dtypes=(F32,), up_offset=None, norm_gain=None, name="mm"):
    mp, ms = xs_p[0].shape[0], xs_s[0].shape[0]
    k = w.shape[1]
    assert sum(x.shape[1] for x in xs_p) == k and mp % bm == 0 and n_cols % bn == 0
    npt = mp // bm
    grid = (n_cols // bn, npt + 1)
    prow = lambda j, i: (jnp.maximum(i - 1, 0), 0)
    ptile = lambda j, i: (jnp.maximum(i - 1, 0), j)
    in_specs = [pl.BlockSpec((bm, x.shape[1]), prow) for x in xs_p]
    in_specs += [pl.BlockSpec((ms, x.shape[1]), lambda j, i: (0, 0)) for x in xs_s]
    in_specs.append(pl.BlockSpec((None, k, bn), lambda j, i: (layer, 0, j)))
    operands = list(xs_p) + list(xs_s) + [w]
    n_w = 1
    if mode == "swiglu":
        off = up_offset // bn
        in_specs.append(pl.BlockSpec((None, k, bn), lambda j, i: (layer, 0, j + off)))
        operands.append(w)
        n_w = 2
    in_specs += [pl.BlockSpec((bm, bn), ptile) for _ in extras_p]
    in_specs += [pl.BlockSpec((ms, bn), lambda j, i: (0, j)) for _ in extras_s]
    operands += list(extras_p) + list(extras_s)
    if norm_gain is not None:
        assert len(xs_p) == 1
        in_specs.append(pl.BlockSpec((1, k), lambda j, i: (0, 0)))
        operands.append(norm_gain)
    out_specs = ([pl.BlockSpec((bm, bn), ptile) for _ in out_dtypes]
                 + [pl.BlockSpec((ms, bn), lambda j, i: (0, j)) for _ in out_dtypes])
    out_shape = ([jax.ShapeDtypeStruct((mp, n_cols), dt) for dt in out_dtypes]
                 + [jax.ShapeDtypeStruct((ms, n_cols), dt) for dt in out_dtypes])
    return pl.pallas_call(
        functools.partial(_mm_kernel, n_x=len(xs_p), n_w=n_w, n_out=len(out_dtypes), mode=mode,
                          norm_x=norm_gain is not None),
        grid=grid,
        in_specs=in_specs,
        out_specs=out_specs,
        out_shape=out_shape,
        scratch_shapes=[pltpu.VMEM((k, bn), BF16) for _ in range(n_w)],
        compiler_params=_cparams(2),
        name=name,
    )(*operands)


def _resident(shape, index_map):
    return pl.BlockSpec(shape, index_map, pipeline_mode=pl.Buffered(1))


def _proj_out_kernel(op_ref, pmp_ref, hp_ref, os_ref, pms_ref, hs_ref, w_ref, g_ref,
                     hp_out, np_out, hs_out, ns_out, wbf):
    i = pl.program_id(0)

    def compute(o_ref, pm_ref, h_ref, h_out, n_out):
        for r0 in range(0, h_ref.shape[0], ROW_SUBTILE):
            rows = slice(r0, r0 + ROW_SUBTILE)
            xb = jnp.concatenate([o_ref[rows, :], pm_ref[rows, :]], axis=1)
            hn = h_ref[rows, :] + jnp.dot(xb, wbf[...], preferred_element_type=F32)
            h_out[rows, :] = hn
            n_out[rows, :] = _rms(hn, g_ref[...]).astype(n_out.dtype)

    @pl.when(i == 0)
    def _():
        wbf[...] = w_ref[...].astype(BF16)
        compute(os_ref, pms_ref, hs_ref, hs_out, ns_out)

    @pl.when(i > 0)
    def _():
        compute(op_ref, pmp_ref, hp_ref, hp_out, np_out)


def _proj_out(o_p, pm_p, h_p, o_s, pm_s, h_s, w_out, g_ffn, layer, bm):
    mp, d = h_p.shape
    ms = h_s.shape[0]
    ko, kp = o_p.shape[1], pm_p.shape[1]
    prow = lambda i: (jnp.maximum(i - 1, 0), 0)
    srow = lambda i: (0, 0)
    return pl.pallas_call(
        _proj_out_kernel,
        grid=(mp // bm + 1,),
        in_specs=[pl.BlockSpec((bm, ko), prow), pl.BlockSpec((bm, kp), prow), pl.BlockSpec((bm, d), prow),
                  pl.BlockSpec((ms, ko), srow), pl.BlockSpec((ms, kp), srow), pl.BlockSpec((ms, d), srow),
                  _resident((None, ko + kp, d), lambda i: (layer, 0, 0)),
                  pl.BlockSpec((None, 1, d), lambda i: (layer, 0, 0))],
        out_specs=[pl.BlockSpec((bm, d), prow), pl.BlockSpec((bm, d), prow),
                   pl.BlockSpec((ms, d), srow), pl.BlockSpec((ms, d), srow)],
        out_shape=[jax.ShapeDtypeStruct((mp, d), F32), jax.ShapeDtypeStruct((mp, d), BF16),
                   jax.ShapeDtypeStruct((ms, d), F32), jax.ShapeDtypeStruct((ms, d), BF16)],
        scratch_shapes=[pltpu.VMEM((ko + kp, d), BF16)],
        compiler_params=_cparams(1),
        name="proj_out",
    )(o_p, pm_p, h_p, o_s, pm_s, h_s, w_out, g_ffn.reshape(-1, 1, d))


def _ple_kernel(*refs, write_h):
    hp_ref, pp_ref, hs_ref, ps_ref, wg_ref, wp_ref, ge_ref, gn_ref = refs[:8]
    outs = refs[8:-2]
    wgb, wpb = refs[-2:]
    if write_h:
        hp_out, np_out, hs_out, ns_out = outs
    else:
        (np_out, ns_out), hp_out, hs_out = outs, None, None
    i = pl.program_id(0)

    def compute(h_ref, p_ref, h_out, n_out):
        for r0 in range(0, h_ref.shape[0], ROW_SUBTILE):
            rows = slice(r0, r0 + ROW_SUBTILE)
            hv = h_ref[rows, :]
            gate = jax.nn.sigmoid(jnp.dot(hv.astype(BF16), wgb[...], preferred_element_type=F32))
            e = _rms(jnp.dot(p_ref[rows, :].astype(BF16), wpb[...], preferred_element_type=F32),
                     ge_ref[...])
            hn = hv + gate * e
            if write_h:
                h_out[rows, :] = hn
            n_out[rows, :] = _rms(hn, gn_ref[...]).astype(n_out.dtype)

    @pl.when(i == 0)
    def _():
        wgb[...] = wg_ref[...].astype(BF16)
        wpb[...] = wp_ref[...].astype(BF16)
        compute(hs_ref, ps_ref, hs_out, ns_out)

    @pl.when(i > 0)
    def _():
        compute(hp_ref, pp_ref, hp_out, np_out)


def _ple(h_p, pe_p, h_s, pe_s, w_gate, w_ple, g_ple, g_next, layer, bm, *, write_h, norm_dtype):
    mp, d = h_p.shape
    ms = h_s.shape[0]
    kp = pe_p.shape[2]
    prow = lambda i: (jnp.maximum(i - 1, 0), 0)
    srow = lambda i: (0, 0)
    n_specs = [pl.BlockSpec((bm, d), prow), pl.BlockSpec((ms, d), srow)]
    n_shapes = [jax.ShapeDtypeStruct((mp, d), norm_dtype), jax.ShapeDtypeStruct((ms, d), norm_dtype)]
    if write_h:
        out_specs = [n_specs[0], n_specs[0], n_specs[1], n_specs[1]]
        out_shape = [jax.ShapeDtypeStruct((mp, d), F32), n_shapes[0],
                     jax.ShapeDtypeStruct((ms, d), F32), n_shapes[1]]
    else:
        out_specs, out_shape = n_specs, n_shapes
    return pl.pallas_call(
        functools.partial(_ple_kernel, write_h=write_h),
        grid=(mp // bm + 1,),
        in_specs=[pl.BlockSpec((bm, d), prow),
                  pl.BlockSpec((None, bm, kp), lambda i: (layer, jnp.maximum(i - 1, 0), 0)),
                  pl.BlockSpec((ms, d), srow),
                  pl.BlockSpec((None, ms, kp), lambda i: (layer, 0, 0)),
                  _resident((None, d, d), lambda i: (layer, 0, 0)),
                  _resident((None, kp, d), lambda i: (layer, 0, 0)),
                  pl.BlockSpec((None, 1, d), lambda i: (layer, 0, 0)),
                  pl.BlockSpec((1, d), lambda i: (0, 0))],
        out_specs=out_specs,
        out_shape=out_shape,
        scratch_shapes=[pltpu.VMEM((d, d), BF16), pltpu.VMEM((kp, d), BF16)],
        compiler_params=_cparams(1),
        name="ple",
    )(h_p, pe_p, h_s, pe_s, w_gate, w_ple, g_ple.reshape(-1, 1, d), g_next)


def _chunk_constants():
    c = CHUNK
    t = np.arange(c)[:, None]
    s = np.arange(c)[None, :]
    masks, signs = [], []
    for h in LEVELS:
        same = (t // (2 * h)) == (s // (2 * h))
        upper = (t % (2 * h)) >= h
        masks.append((same & upper & ((s % (2 * h)) < h)).astype(np.float32))
        signs.append(np.broadcast_to(np.where(upper, LOG2E, -LOG2E),
                                     (c, HEADS_PER_STEP * HGRN_DK)).astype(np.float32))
    tri = (s <= t).astype(np.float32)
    return (jnp.asarray(tri, dtype=BF16), jnp.asarray(np.stack(masks), dtype=BF16),
            jnp.asarray(np.stack(signs), dtype=F32))


def _lower_bound(lb_ref, layer):
    lg = lb_ref[...]
    e = jnp.exp(lg - jnp.max(lg, axis=0, keepdims=True))
    s = e / jnp.sum(e, axis=0, keepdims=True)
    c = s[0:1, :]
    for r in range(1, layer + 1):
        c = c + s[r:r + 1, :]
    return c - s[0:1, :]


def _gates(q_raw, f_raw, lb):
    q = _silu(q_raw) * (HGRN_DK ** -0.5)
    fg = lb + (1.0 - lb) * jax.nn.sigmoid(f_raw)
    return q, 1.0 - fg, fg


def _split2(x):
    hi = x.astype(BF16)
    lo = (x - hi.astype(F32)).astype(BF16)
    return jnp.concatenate([hi, lo], axis=1)


def _boundary_rows(b, h):
    n_grp = CHUNK // SUBLANES

    def bcast(grp, sub):
        r = grp * SUBLANES + sub
        return jnp.broadcast_to(b[r:r + 1, :], (SUBLANES, b.shape[1]))

    sub_id = lax.broadcasted_iota(jnp.int32, (SUBLANES, b.shape[1]), 0)
    pieces = []
    for grp in range(n_grp):
        if h >= SUBLANES:
            per_blk = 2 * h // SUBLANES
            pieces.append(bcast((grp // per_blk) * per_blk + per_blk // 2 - 1, SUBLANES - 1))
        else:
            piece = bcast(grp, h - 1)
            for blk in range(1, SUBLANES // (2 * h)):
                piece = jnp.where(sub_id < blk * 2 * h, piece, bcast(grp, blk * 2 * h + h - 1))
            pieces.append(piece)
    return jnp.concatenate(pieces, axis=0)


_NT = (((1,), (1,)), ((), ()))


def _hgrn_chunk(q_raw, f_raw, v, lb, tri_ref, mask_ref, sign_ref, st_ref):
    c, w = CHUNK, q_raw.shape[1]
    heads = [slice(i * HGRN_DK, (i + 1) * HGRN_DK) for i in range(w // HGRN_DK)]
    q, k, fg = _gates(q_raw, f_raw, lb)
    b2 = jnp.dot(tri_ref[...], _split2(jnp.log(fg)), preferred_element_type=F32)
    b = b2[:, :w] + b2[:, w:]
    q16, k16, v16 = q.astype(BF16), k.astype(BF16), v.astype(BF16)
    a = [jnp.zeros((c, c), BF16) for _ in heads]
    for li, h in enumerate(LEVELS):
        if h == 1:
            e = jnp.where(sign_ref[li] > 0.0, fg, 1.0).astype(BF16)
        else:
            e = jnp.exp2((b - _boundary_rows(b, h)) * sign_ref[li]).astype(BF16)
        qe, ke = q16 * e, k16 * e
        for i, cs in enumerate(heads):
            ah = lax.dot_general(qe[:, cs], ke[:, cs], _NT, preferred_element_type=F32)
            a[i] = a[i] + ah.astype(BF16) * mask_ref[li]
    qk = q * k
    qb = (q * jnp.exp(b)).astype(BF16)
    bl = b[c - 1:c, :]
    kd = (k * jnp.exp(bl - b)).astype(BF16)
    el = jnp.exp(bl)
    outs = []
    for i, cs in enumerate(heads):
        st = st_ref[i]
        o = jnp.dot(a[i], v16[:, cs], preferred_element_type=F32)
        o = o + jnp.sum(qk[:, cs], axis=1, keepdims=True) * v[:, cs]
        o = o + lax.dot_general(qb[:, cs], st.astype(BF16), _NT, preferred_element_type=F32)
        st_ref[i] = st * el[:, cs] + jnp.dot(v[:, cs].T.astype(BF16), kd[:, cs],
                                             preferred_element_type=F32)
        outs.append(o)
    return outs


def _hgrn_kernel(q_ref, f_ref, v_ref, g_ref, lb_ref, gh_ref, tri_ref, mask_ref, sign_ref,
                 qd_ref, fd_ref, vd_ref, gd_ref, lbd_ref, sd_ref, *rest, layer, n_chunks, first_call):
    o_ref, s_ref, od_ref, sn_ref, st_ref, kdec, fdec, qfdec, orow = rest[-9:]
    t = pl.program_id(2)
    n_dec = qd_ref.shape[0]
    per = n_dec // n_chunks
    depth = sn_ref.shape[0] if first_call else None

    @pl.when(t == 0)
    def _():
        st_ref[...] = jnp.zeros_like(st_ref)

    lb = _lower_bound(lb_ref, layer)
    gh = gh_ref[layer:layer + 1, :]

    qd, kd, fgd = _gates(qd_ref[...], fd_ref[...], _lower_bound(lbd_ref, layer))
    kdec[...] = kd
    fdec[...] = fgd
    qfdec[...] = qd * fgd
    qk_dec = jnp.sum(qd * kd, axis=1, keepdims=True)
    if first_call:
        for other in range(depth):
            if other != layer:
                sn_ref[other] = jnp.zeros(sn_ref.shape[1:], F32)
    pad = jnp.zeros((HGRN_DK - per, HGRN_DK), F32)

    def body(ci, carry):
        rows = pl.ds(pl.multiple_of(ci * CHUNK, CHUNK), CHUNK)
        outs = _hgrn_chunk(q_ref[rows, :], f_ref[rows, :], v_ref[rows, :], lb, tri_ref, mask_ref,
                           sign_ref, st_ref)
        o = jnp.concatenate([_rms(oh, gh) for oh in outs], axis=1)
        o_ref[rows, :] = (o * _silu(g_ref[rows, :])).astype(o_ref.dtype)

        r0 = pl.multiple_of(ci * per, per)
        grp = pl.ds(r0, per)
        ft = jnp.concatenate([fdec[grp, :], pad], axis=0).T
        kt = jnp.concatenate([kdec[grp, :], pad], axis=0).T
        v_grp = vd_ref[grp, :]
        qf_grp = qfdec[grp, :].astype(BF16)
        o_rows = []
        for r in range(per):
            s_old = sd_ref[r0 + r]
            s_new = ft[:, r:r + 1] * s_old + kt[:, r:r + 1] * v_grp[r:r + 1, :]
            if first_call:
                sn_ref[layer, r0 + r] = s_new
            else:
                sn_ref[r0 + r] = s_new
            o_rows.append(jnp.dot(qf_grp[r:r + 1, :], s_old.astype(BF16), preferred_element_type=F32))
        orow[grp, :] = jnp.concatenate(o_rows, axis=0)
        return carry

    lax.fori_loop(0, n_chunks, body, 0, unroll=2)
    o_dec = orow[...] + qk_dec * vd_ref[...]
    od_ref[...] = (_rms(o_dec, gh) * _silu(gd_ref[...])).astype(od_ref.dtype)

    @pl.when(t == pl.num_programs(2) - 1)
    def _():
        for hh in range(HEADS_PER_STEP):
            s_ref[hh] = st_ref[hh].T


def _hgrn(proj_p, proj_s, state, lb_logits, g_head, consts, layer, batch, seq, heads, new_state=None):
    hp = HEADS_PER_STEP
    n_hp = heads // hp
    wid = hp * HGRN_DK
    nt = seq // SEQ_TILE
    n_chunks = SEQ_TILE // CHUNK
    depth, dec_batch = state.shape[0], state.shape[1]
    n_dec = dec_batch // nt
    assert batch * n_hp == heads and dec_batch % nt == 0 and n_dec // n_chunks == SUBLANES
    first_call = new_state is None
    dhead = lambda b, p: b * n_hp + p
    pcol = lambda grp: (lambda b, p, t: (b * nt + t, grp * n_hp + p))
    dcol = lambda grp: (lambda b, p, t: (t, grp * heads + dhead(b, p)))
    tri, masks, signs = consts
    in_specs = [pl.BlockSpec((SEQ_TILE, wid), pcol(0)),
                pl.BlockSpec((SEQ_TILE, wid), pcol(1)),
                pl.BlockSpec((SEQ_TILE, wid), pcol(2)),
                pl.BlockSpec((SEQ_TILE, wid), pcol(3)),
                pl.BlockSpec((depth, wid), lambda b, p, t: (0, p)),
                pl.BlockSpec((depth, HGRN_DV), lambda b, p, t: (0, 0)),
                pl.BlockSpec(tri.shape, lambda b, p, t: (0, 0)),
                pl.BlockSpec(masks.shape, lambda b, p, t: (0, 0, 0)),
                pl.BlockSpec(signs.shape, lambda b, p, t: (0, 0, 0)),
                pl.BlockSpec((n_dec, HGRN_DK), dcol(0)),
                pl.BlockSpec((n_dec, HGRN_DK), dcol(1)),
                pl.BlockSpec((n_dec, HGRN_DV), dcol(2)),
                pl.BlockSpec((n_dec, HGRN_DV), dcol(3)),
                pl.BlockSpec((depth, HGRN_DK), lambda b, p, t: (0, dhead(b, p))),
                pl.BlockSpec((None, n_dec, None, HGRN_DK, HGRN_DV),
                             lambda b, p, t: (layer, t, dhead(b, p), 0, 0))]
    operands = [proj_p, proj_p, proj_p, proj_p, lb_logits, g_head, tri, masks, signs,
                proj_s, proj_s, proj_s, proj_s, lb_logits, state]
    aliases = {}
    if first_call:
        state_out = pl.BlockSpec((depth, n_dec, None, HGRN_DK, HGRN_DV),
                                 lambda b, p, t: (0, t, dhead(b, p), 0, 0))
    else:
        state_out = pl.BlockSpec((None, n_dec, None, HGRN_DK, HGRN_DV),
                                 lambda b, p, t: (layer, t, dhead(b, p), 0, 0))
        in_specs.append(pl.BlockSpec(memory_space=pl.ANY))
        operands.append(new_state)
        aliases = {len(operands) - 1: 3}
    return pl.pallas_call(
        functools.partial(_hgrn_kernel, layer=layer, n_chunks=n_chunks, first_call=first_call),
        grid=(batch, n_hp, nt),
        in_specs=in_specs,
        out_specs=[pl.BlockSpec((SEQ_TILE, wid), lambda b, p, t: (b * nt + t, p)),
                   pl.BlockSpec((None, hp, HGRN_DK, HGRN_DV), lambda b, p, t: (b, p, 0, 0)),
                   pl.BlockSpec((n_dec, HGRN_DV), lambda b, p, t: (t, dhead(b, p))),
                   state_out],
        out_shape=[jax.ShapeDtypeStruct((batch * seq, heads * HGRN_DV), BF16),
                   jax.ShapeDtypeStruct((batch, heads, HGRN_DK, HGRN_DV), F32),
                   jax.ShapeDtypeStruct((dec_batch, heads * HGRN_DV), BF16),
                   jax.ShapeDtypeStruct(state.shape, F32)],
        scratch_shapes=[pltpu.VMEM((hp, HGRN_DV, HGRN_DK), F32),
                        pltpu.VMEM((n_dec, HGRN_DK), F32), pltpu.VMEM((n_dec, HGRN_DK), F32),
                        pltpu.VMEM((n_dec, HGRN_DK), F32), pltpu.VMEM((n_dec, HGRN_DV), F32)],
        input_output_aliases=aliases,
        compiler_params=_cparams(3),
        name="hgrn",
    )(*operands)


POOL_CARRY = 16


def _pool_mix(pooled, wp_ref, ps_ref, o_ref, d_pg):
    for gi in range(len(POOL_WINDOWS)):
        cs = slice(gi * d_pg, (gi + 1) * d_pg)
        pm = jnp.dot(pooled[gi].astype(BF16), wp_ref[gi].astype(BF16), preferred_element_type=F32)
        o_ref[:, cs] = (pm * ps_ref[:, cs]).astype(o_ref.dtype)


def _pool_prompt_kernel(u_ref, wp_ref, ps_ref, o_ref, ext_ref, *, tt, d_pg):
    t = pl.program_id(1)

    @pl.when(t == 0)
    def _():
        ext_ref[0:POOL_CARRY, :] = jnp.zeros((POOL_CARRY, ext_ref.shape[1]), F32)

    @pl.when(t > 0)
    def _():
        ext_ref[0:POOL_CARRY, :] = ext_ref[tt:tt + POOL_CARRY, :]

    ext_ref[POOL_CARRY:POOL_CARRY + tt, :] = u_ref[...]
    pos = t * tt + lax.broadcasted_iota(jnp.int32, (tt, 1), 0)
    pooled = []
    for gi, w in enumerate(POOL_WINDOWS):
        cs = slice(gi * d_pg, (gi + 1) * d_pg)
        s = ext_ref[:, cs]
        span = 1
        while span < w:
            s = s + pltpu.roll(s, span, 0)
            span *= 2
        inv_cnt = 1.0 / jnp.minimum(pos + 1, w).astype(F32)
        pooled.append(s[POOL_CARRY:, :] * inv_cnt - u_ref[:, cs])
    _pool_mix(pooled, wp_ref, ps_ref, o_ref, d_pg)


def _pool_prompt(proj, w_pool, pool_scale, layer, batch, seq, u_col_blk, tt):
    d_pool = pool_scale.shape[1]
    n_pool, d_pg = w_pool.shape[1], w_pool.shape[2]
    nt = seq // tt
    return pl.pallas_call(
        functools.partial(_pool_prompt_kernel, tt=tt, d_pg=d_pg),
        grid=(batch, nt),
        in_specs=[pl.BlockSpec((tt, d_pool), lambda b, t: (b * nt + t, u_col_blk)),
                  pl.BlockSpec((None, n_pool, d_pg, d_pg), lambda b, t: (layer, 0, 0, 0)),
                  pl.BlockSpec((None, 1, d_pool), lambda b, t: (layer, 0, 0))],
        out_specs=pl.BlockSpec((tt, d_pool), lambda b, t: (b * nt + t, 0)),
        out_shape=jax.ShapeDtypeStruct((batch * seq, d_pool), BF16),
        scratch_shapes=[pltpu.VMEM((tt + POOL_CARRY, d_pool), F32)],
        compiler_params=_cparams(2),
        name="pool_prompt",
    )(proj, w_pool, pool_scale.reshape(-1, 1, d_pool))


def _pool_sample_kernel(u_ref, buf_ref, wp_ref, ps_ref, o_ref, *, d_pg, n_buf):
    pooled = []
    for gi, w in enumerate(POOL_WINDOWS):
        cs = slice(gi * d_pg, (gi + 1) * d_pg)
        u = u_ref[:, cs]
        s = u
        for j in range(1, w):
            s = s + buf_ref[n_buf - j, :, cs]
        cnt = float(min(PAST_LEN + 1, w))
        pooled.append(s / cnt - u)
    _pool_mix(pooled, wp_ref, ps_ref, o_ref, d_pg)


def _pool_sample(proj, buf_t, w_pool, pool_scale, layer, u_col_blk, dec_batch):
    d_pool = pool_scale.shape[1]
    n_pool, d_pg = w_pool.shape[1], w_pool.shape[2]
    n_buf = buf_t.shape[0]
    return pl.pallas_call(
        functools.partial(_pool_sample_kernel, d_pg=d_pg, n_buf=n_buf),
        grid=(1,),
        in_specs=[pl.BlockSpec((dec_batch, d_pool), lambda i: (0, u_col_blk)),
                  pl.BlockSpec(buf_t.shape, lambda i: (0, 0, 0)),
                  pl.BlockSpec((None, n_pool, d_pg, d_pg), lambda i: (layer, 0, 0, 0)),
                  pl.BlockSpec((None, 1, d_pool), lambda i: (layer, 0, 0))],
        out_specs=pl.BlockSpec((dec_batch, d_pool), lambda i: (0, 0)),
        out_shape=jax.ShapeDtypeStruct((dec_batch, d_pool), BF16),
        compiler_params=_cparams(1),
        name="pool_sample",
    )(proj, buf_t, w_pool, pool_scale.reshape(-1, 1, d_pool))


def kernel(x_prompt, x_sample, state_hgrn, state_pool, p_prompt, p_sample, g_mix, w_in, lb_logits,
           g_head, w_pool, pool_scale, w_out, g_ffn, w_gate_up, w_down, w_ple, g_ple, w_ple_gate,
           g_final):
    batch, seq, d_model = x_prompt.shape
    dec_batch = x_sample.shape[0]
    depth = w_in.shape[0]
    heads = lb_logits.shape[1] // HGRN_DK
    d_hgrn = heads * HGRN_DV
    d_pool = pool_scale.shape[1]
    d_ff = w_down.shape[1]
    pool_buf = state_pool.shape[2]
    mp = batch * seq
    assert x_sample.shape[1] == 1 and seq % SEQ_TILE == 0 and heads % HEADS_PER_STEP == 0
    assert dec_batch % ROW_SUBTILE == 0
    assert w_in.shape[2] == 2 * heads * HGRN_DK + 2 * d_hgrn + d_pool

    bm = 1024
    bm_row = 512
    u_col_blk = (2 * heads * HGRN_DK + 2 * d_hgrn) // d_pool
    consts = _chunk_constants()

    h_p = x_prompt.reshape(mp, d_model)
    h_s = x_sample.reshape(dec_batch, d_model)
    pe_p = p_prompt.reshape(depth, mp, -1)
    pe_s = p_sample.reshape(depth, dec_batch, -1)

    s_prompt, pool_prompt, pool_sample = [], [], []
    new_state = None
    n_p, n_s = h_p, h_s
    for l in range(depth):
        proj_p, proj_s = _matmul([n_p], [n_s], w_in, l, n_cols=w_in.shape[2], bm=bm, bn=1024,
                                 norm_gain=g_mix[0].reshape(1, d_model) if l == 0 else None,
                                 name="proj_in")

        o_p, st_p, o_s, new_state = _hgrn(proj_p, proj_s, state_hgrn, lb_logits, g_head, consts, l,
                                          batch, seq, heads, new_state)
        pm_p = _pool_prompt(proj_p, w_pool, pool_scale, l, batch, seq, u_col_blk, tt=512)
        buf_t = jnp.transpose(state_pool[l], (1, 0, 2))
        pm_s = _pool_sample(proj_s, buf_t, w_pool, pool_scale, l, u_col_blk, dec_batch)
        s_prompt.append(st_p)
        u_off = u_col_blk * d_pool
        pool_prompt.append(proj_p.reshape(batch, seq, -1)[:, seq - pool_buf:, u_off:])
        pool_sample.append(jnp.concatenate([state_pool[l][:, 1:], proj_s[:, None, u_off:]], axis=1))

        h_p, n_p, h_s, n_s = _proj_out(o_p, pm_p, h_p, o_s, pm_s, h_s, w_out, g_ffn, l, bm_row)
        act_p, act_s = _matmul([n_p], [n_s], w_gate_up, l, n_cols=d_ff, bm=bm, bn=512,
                               mode="swiglu", up_offset=d_ff, out_dtypes=(BF16,), name="ffn_up")
        h_p, h_s = _matmul([act_p], [act_s], w_down, l, n_cols=d_model, bm=bm // 2, bn=512,
                           mode="res", extras_p=(h_p,), extras_s=(h_s,), name="ffn_down")
        if l + 1 < depth:
            h_p, n_p, h_s, n_s = _ple(h_p, pe_p, h_s, pe_s, w_ple_gate, w_ple, g_ple,
                                      g_mix[l + 1].reshape(1, d_model), l, bm_row,
                                      write_h=True, norm_dtype=BF16)
        else:
            y_p, y_s = _ple(h_p, pe_p, h_s, pe_s, w_ple_gate, w_ple, g_ple,
                            g_final.reshape(1, d_model), l, bm_row, write_h=False, norm_dtype=F32)

    return (y_p.reshape(batch, seq, d_model), y_s.reshape(dec_batch, 1, d_model),
            jnp.stack(s_prompt), jnp.stack(pool_prompt), new_state, jnp.stack(pool_sample))
```

```python
import functools

import numpy as np
import jax
import jax.numpy as jnp
from jax import lax
from jax.experimental import pallas as pl
from jax.experimental.pallas import tpu as pltpu

F32 = jnp.float32
BF16 = jnp.bfloat16

EPS = 1e-6
LOG2E = 1.4426950408889634
HGRN_DK = 128
HGRN_DV = 128
POOL_WINDOWS = (2, 4, 8, 16)
PAST_LEN = 16384
SUBLANES = 8
CHUNK = 128
LEVELS = (64, 32, 16, 8, 4, 2, 1)
HEADS_PER_STEP = 4
SEQ_TILE = 1024
ROW_SUBTILE = 128
POOL_CARRY = 16

V7X_VMEM_BYTES = 64 * 1024 * 1024
VMEM_LIMIT_BYTES = V7X_VMEM_BYTES - 8 * 1024 * 1024


def _cparams(n_axes):
    return pltpu.CompilerParams(
        dimension_semantics=("arbitrary",) * n_axes, vmem_limit_bytes=VMEM_LIMIT_BYTES)


def _silu(x):
    hx = 0.5 * x
    return hx + hx * jnp.tanh(hx)


def _rms(x, g):
    return x * lax.rsqrt(jnp.mean(x * x, axis=-1, keepdims=True) + EPS) * g


_N_EXTRA = {"plain": 0, "res": 1, "swiglu": 0}


def _mm_kernel(*refs, n_x, n_w, n_out, mode, norm_x):
    n_e = _N_EXTRA[mode]
    pos = 0

    def take(n):
        nonlocal pos
        out = refs[pos:pos + n]
        pos += n
        return out

    xp, xs, w_refs = take(n_x), take(n_x), take(n_w)
    ep, es, gain = take(n_e), take(n_e), take(1 if norm_x else 0)
    op, osm, wbf = take(n_out), take(n_out), take(n_w)
    i = pl.program_id(1)

    @pl.when(i == 0)
    def _():
        for w_ref, wb in zip(w_refs, wbf):
            wb[...] = w_ref[...].astype(BF16)

    def compute(x_refs, extra, outs):
        if norm_x:
            cast = [_rms(x_refs[0][...], gain[0][...]).astype(BF16)]
        else:
            cast = [xr[...].astype(BF16) for xr in x_refs]
        xb = cast[0] if n_x == 1 else jnp.concatenate(cast, axis=1)
        acc = jnp.dot(xb, wbf[0][...], preferred_element_type=F32)
        if mode == "plain":
            res = acc
        elif mode == "res":
            res = extra[0][...] + acc
        else:
            res = _silu(acc) * jnp.dot(xb, wbf[1][...], preferred_element_type=F32)
        for o_ref in outs:
            o_ref[...] = res.astype(o_ref.dtype)

    @pl.when(i == 0)
    def _():
        compute(xs, es, osm)

    @pl.when(i > 0)
    def _():
        compute(xp, ep, op)


def _matmul(xs_p, xs_s, w, layer, *, n_cols, bm, bn, mode="plain", extras_p=(), extras_s=(),
            out_dtypes=(F32,), up_offset=None, norm_gain=None, name="mm"):
    mp, ms = xs_p[0].shape[0], xs_s[0].shape[0]
    k = w.shape[1]
    assert sum(x.shape[1] for x in xs_p) == k and mp % bm == 0 and n_cols % bn == 0
    npt = mp // bm
    grid = (n_cols // bn, npt + 1)
    prow = lambda j, i: (jnp.maximum(i - 1, 0), 0)
    ptile = lambda j, i: (jnp.maximum(i - 1, 0), j)
    in_specs = [pl.BlockSpec((bm, x.shape[1]), prow) for x in xs_p]
    in_specs += [pl.BlockSpec((ms, x.shape[1]), lambda j, i: (0, 0)) for x in xs_s]
    in_specs.append(pl.BlockSpec((None, k, bn), lambda j, i: (layer, 0, j)))
    operands = list(xs_p) + list(xs_s) + [w]
    n_w = 1
    if mode == "swiglu":
        off = up_offset // bn
        in_specs.append(pl.BlockSpec((None, k, bn), lambda j, i: (layer, 0, j + off)))
        operands.append(w)
        n_w = 2
    in_specs += [pl.BlockSpec((bm, bn), ptile) for _ in extras_p]
    in_specs += [pl.BlockSpec((ms, bn), lambda j, i: (0, j)) for _ in extras_s]
    operands += list(extras_p) + list(extras_s)
    if norm_gain is not None:
        assert len(xs_p) == 1
        in_specs.append(pl.BlockSpec((1, k), lambda j, i: (0, 0)))
        operands.append(norm_gain)
    out_specs = ([pl.BlockSpec((bm, bn), ptile) for _ in out_dtypes]
                 + [pl.BlockSpec((ms, bn), lambda j, i: (0, j)) for _ in out_dtypes])
    out_shape = ([jax.ShapeDtypeStruct((mp, n_cols), dt) for dt in out_dtypes]
                 + [jax.ShapeDtypeStruct((ms, n_cols), dt) for dt in out_dtypes])
    return pl.pallas_call(
        functools.partial(_mm_kernel, n_x=len(xs_p), n_w=n_w, n_out=len(out_dtypes), mode=mode,
                          norm_x=norm_gain is not None),
        grid=grid,
        in_specs=in_specs,
        out_specs=out_specs,
        out_shape=out_shape,
        scratch_shapes=[pltpu.VMEM((k, bn), BF16) for _ in range(n_w)],
        compiler_params=_cparams(2),
        name=name,
    )(*operands)


def _resident(shape, index_map):
    return pl.BlockSpec(shape, index_map, pipeline_mode=pl.Buffered(1))


def _proj_out_kernel(op_ref, up_ref, hp_ref, os_ref, pms_ref, hs_ref, w_ref, wp_ref, ps_ref, g_ref,
                     hp_out, np_out, hs_out, ns_out, wbf, wpb, ext_ref, *, tiles_per_seq):
    i = pl.program_id(0)
    bm = hp_ref.shape[0]
    d_pg = wp_ref.shape[-1]

    def project(o, pm, h, h_out, n_out, rows):
        xb = jnp.concatenate([o, pm], axis=1)
        hn = h + jnp.dot(xb, wbf[...], preferred_element_type=F32)
        h_out[rows, :] = hn
        n_out[rows, :] = _rms(hn, g_ref[...]).astype(n_out.dtype)

    @pl.when(i == 0)
    def _():
        wbf[...] = w_ref[...].astype(BF16)
        wpb[...] = wp_ref[...].astype(BF16)
        project(os_ref[...], pms_ref[...], hs_ref[...], hs_out, ns_out, slice(None))

    @pl.when(i > 0)
    def _():
        t_in_seq = lax.rem(i - 1, tiles_per_seq)

        @pl.when(t_in_seq == 0)
        def _():
            ext_ref[0:POOL_CARRY, :] = jnp.zeros((POOL_CARRY, ext_ref.shape[1]), F32)

        @pl.when(t_in_seq > 0)
        def _():
            ext_ref[0:POOL_CARRY, :] = ext_ref[bm:bm + POOL_CARRY, :]

        ext_ref[POOL_CARRY:POOL_CARRY + bm, :] = up_ref[...]
        for r0 in range(0, bm, ROW_SUBTILE):
            rows = slice(r0, r0 + ROW_SUBTILE)
            pos = t_in_seq * bm + r0 + lax.broadcasted_iota(jnp.int32, (ROW_SUBTILE, 1), 0)
            pm = []
            for gi, w in enumerate(POOL_WINDOWS):
                cs = slice(gi * d_pg, (gi + 1) * d_pg)
                s = ext_ref[r0:r0 + POOL_CARRY + ROW_SUBTILE, cs]
                span = 1
                while span < w:
                    s = s + pltpu.roll(s, span, 0)
                    span *= 2
                inv_cnt = 1.0 / jnp.minimum(pos + 1, w).astype(F32)
                pooled = s[POOL_CARRY:, :] * inv_cnt - up_ref[rows, cs]
                mixed = jnp.dot(pooled.astype(BF16), wpb[gi], preferred_element_type=F32)
                pm.append((mixed * ps_ref[:, cs]).astype(BF16))
            project(op_ref[rows, :], jnp.concatenate(pm, axis=1), hp_ref[rows, :], hp_out, np_out, rows)


def _proj_out(o_p, proj_p, h_p, o_s, pm_s, h_s, w_out, w_pool, pool_scale, g_ffn, layer, bm, seq,
              u_col_blk):
    mp, d = h_p.shape
    ms = h_s.shape[0]
    ko, kp = o_p.shape[1], pm_s.shape[1]
    n_pool, d_pg = w_pool.shape[1], w_pool.shape[2]
    assert seq % bm == 0 and bm % ROW_SUBTILE == 0
    prow = lambda i: (jnp.maximum(i - 1, 0), 0)
    srow = lambda i: (0, 0)
    return pl.pallas_call(
        functools.partial(_proj_out_kernel, tiles_per_seq=seq // bm),
        grid=(mp // bm + 1,),
        in_specs=[pl.BlockSpec((bm, ko), prow),
                  pl.BlockSpec((bm, kp), lambda i: (jnp.maximum(i - 1, 0), u_col_blk)),
                  pl.BlockSpec((bm, d), prow),
                  pl.BlockSpec((ms, ko), srow), pl.BlockSpec((ms, kp), srow), pl.BlockSpec((ms, d), srow),
                  _resident((None, ko + kp, d), lambda i: (layer, 0, 0)),
                  pl.BlockSpec((None, n_pool, d_pg, d_pg), lambda i: (layer, 0, 0, 0)),
                  pl.BlockSpec((None, 1, kp), lambda i: (layer, 0, 0)),
                  pl.BlockSpec((None, 1, d), lambda i: (layer, 0, 0))],
        out_specs=[pl.BlockSpec((bm, d), prow), pl.BlockSpec((bm, d), prow),
                   pl.BlockSpec((ms, d), srow), pl.BlockSpec((ms, d), srow)],
        out_shape=[jax.ShapeDtypeStruct((mp, d), F32), jax.ShapeDtypeStruct((mp, d), BF16),
                   jax.ShapeDtypeStruct((ms, d), F32), jax.ShapeDtypeStruct((ms, d), BF16)],
        scratch_shapes=[pltpu.VMEM((ko + kp, d), BF16), pltpu.VMEM((n_pool, d_pg, d_pg), BF16),
                        pltpu.VMEM((bm + POOL_CARRY, kp), F32)],
        compiler_params=_cparams(1),
        name="proj_out",
    )(o_p, proj_p, h_p, o_s, pm_s, h_s, w_out, w_pool, pool_scale.reshape(-1, 1, kp),
      g_ffn.reshape(-1, 1, d))


def _ple_kernel(*refs, write_h):
    hp_ref, pp_ref, hs_ref, ps_ref, wg_ref, wp_ref, ge_ref, gn_ref = refs[:8]
    outs = refs[8:-2]
    wgb, wpb = refs[-2:]
    if write_h:
        hp_out, np_out, hs_out, ns_out = outs
    else:
        (np_out, ns_out), hp_out, hs_out = outs, None, None
    i = pl.program_id(0)

    def compute(h_ref, p_ref, h_out, n_out):
        for r0 in range(0, h_ref.shape[0], ROW_SUBTILE):
            rows = slice(r0, r0 + ROW_SUBTILE)
            hv = h_ref[rows, :]
            gate = jax.nn.sigmoid(jnp.dot(hv.astype(BF16), wgb[...], preferred_element_type=F32))
            e = _rms(jnp.dot(p_ref[rows, :].astype(BF16), wpb[...], preferred_element_type=F32),
                     ge_ref[...])
            hn = hv + gate * e
            if write_h:
                h_out[rows, :] = hn
            n_out[rows, :] = _rms(hn, gn_ref[...]).astype(n_out.dtype)

    @pl.when(i == 0)
    def _():
        wgb[...] = wg_ref[...].astype(BF16)
        wpb[...] = wp_ref[...].astype(BF16)
        compute(hs_ref, ps_ref, hs_out, ns_out)

    @pl.when(i > 0)
    def _():
        compute(hp_ref, pp_ref, hp_out, np_out)


def _ple(h_p, pe_p, h_s, pe_s, w_gate, w_ple, g_ple, g_next, layer, bm, *, write_h, norm_dtype):
    mp, d = h_p.shape
    ms = h_s.shape[0]
    kp = pe_p.shape[2]
    prow = lambda i: (jnp.maximum(i - 1, 0), 0)
    srow = lambda i: (0, 0)
    n_specs = [pl.BlockSpec((bm, d), prow), pl.BlockSpec((ms, d), srow)]
    n_shapes = [jax.ShapeDtypeStruct((mp, d), norm_dtype), jax.ShapeDtypeStruct((ms, d), norm_dtype)]
    if write_h:
        out_specs = [n_specs[0], n_specs[0], n_specs[1], n_specs[1]]
        out_shape = [jax.ShapeDtypeStruct((mp, d), F32), n_shapes[0],
                     jax.ShapeDtypeStruct((ms, d), F32), n_shapes[1]]
    else:
        out_specs, out_shape = n_specs, n_shapes
    return pl.pallas_call(
        functools.partial(_ple_kernel, write_h=write_h),
        grid=(mp // bm + 1,),
        in_specs=[pl.BlockSpec((bm, d), prow),
                  pl.BlockSpec((None, bm, kp), lambda i: (layer, jnp.maximum(i - 1, 0), 0)),
                  pl.BlockSpec((ms, d), srow),
                  pl.BlockSpec((None, ms, kp), lambda i: (layer, 0, 0)),
                  _resident((None, d, d), lambda i: (layer, 0, 0)),
                  _resident((None, kp, d), lambda i: (layer, 0, 0)),
                  pl.BlockSpec((None, 1, d), lambda i: (layer, 0, 0)),
                  pl.BlockSpec((1, d), lambda i: (0, 0))],
        out_specs=out_specs,
        out_shape=out_shape,
        scratch_shapes=[pltpu.VMEM((d, d), BF16), pltpu.VMEM((kp, d), BF16)],
        compiler_params=_cparams(1),
        name="ple",
    )(h_p, pe_p, h_s, pe_s, w_gate, w_ple, g_ple.reshape(-1, 1, d), g_next)


def _chunk_constants():
    c = CHUNK
    t = np.arange(c)[:, None]
    s = np.arange(c)[None, :]
    masks, signs = [], []
    for h in LEVELS:
        same = (t // (2 * h)) == (s // (2 * h))
        upper = (t % (2 * h)) >= h
        masks.append((same & upper & ((s % (2 * h)) < h)).astype(np.float32))
        signs.append(np.broadcast_to(np.where(upper, LOG2E, -LOG2E),
                                     (c, HEADS_PER_STEP * HGRN_DK)).astype(np.float32))
    tri = (s <= t).astype(np.float32)
    return (jnp.asarray(tri, dtype=BF16), jnp.asarray(np.stack(masks), dtype=BF16),
            jnp.asarray(np.stack(signs), dtype=F32))


def _lower_bound(lb_ref, layer):
    lg = lb_ref[...]
    e = jnp.exp(lg - jnp.max(lg, axis=0, keepdims=True))
    s = e / jnp.sum(e, axis=0, keepdims=True)
    c = s[0:1, :]
    for r in range(1, layer + 1):
        c = c + s[r:r + 1, :]
    return c - s[0:1, :]


def _gates(q_raw, f_raw, lb):
    q = _silu(q_raw) * (HGRN_DK ** -0.5)
    fg = lb + (1.0 - lb) * jax.nn.sigmoid(f_raw)
    return q, 1.0 - fg, fg


def _split2(x):
    hi = x.astype(BF16)
    lo = (x - hi.astype(F32)).astype(BF16)
    return jnp.concatenate([hi, lo], axis=1)


def _boundary_rows(b, h):
    n_grp = CHUNK // SUBLANES

    def bcast(grp, sub):
        r = grp * SUBLANES + sub
        return jnp.broadcast_to(b[r:r + 1, :], (SUBLANES, b.shape[1]))

    sub_id = lax.broadcasted_iota(jnp.int32, (SUBLANES, b.shape[1]), 0)
    pieces = []
    for grp in range(n_grp):
        if h >= SUBLANES:
            per_blk = 2 * h // SUBLANES
            pieces.append(bcast((grp // per_blk) * per_blk + per_blk // 2 - 1, SUBLANES - 1))
        else:
            piece = bcast(grp, h - 1)
            for blk in range(1, SUBLANES // (2 * h)):
                piece = jnp.where(sub_id < blk * 2 * h, piece, bcast(grp, blk * 2 * h + h - 1))
            pieces.append(piece)
    return jnp.concatenate(pieces, axis=0)


_NT = (((1,), (1,)), ((), ()))


def _hgrn_chunk(q_raw, f_raw, v, lb, tri_ref, mask_ref, sign_ref, st_ref):
    c, w = CHUNK, q_raw.shape[1]
    heads = [slice(i * HGRN_DK, (i + 1) * HGRN_DK) for i in range(w // HGRN_DK)]
    q, k, fg = _gates(q_raw, f_raw, lb)
    b2 = jnp.dot(tri_ref[...], _split2(jnp.log(fg)), preferred_element_type=F32)
    b = b2[:, :w] + b2[:, w:]
    q16, k16, v16 = q.astype(BF16), k.astype(BF16), v.astype(BF16)
    a = [jnp.zeros((c, c), BF16) for _ in heads]
    for li, h in enumerate(LEVELS):
        if h == 1:
            e = jnp.where(sign_ref[li] > 0.0, fg, 1.0).astype(BF16)
        else:
            e = jnp.exp2((b - _boundary_rows(b, h)) * sign_ref[li]).astype(BF16)
        qe, ke = q16 * e, k16 * e
        for i, cs in enumerate(heads):
            ah = lax.dot_general(qe[:, cs], ke[:, cs], _NT, preferred_element_type=F32)
            a[i] = a[i] + ah.astype(BF16) * mask_ref[li]
    qk = q * k
    qb = (q * jnp.exp(b)).astype(BF16)
    bl = b[c - 1:c, :]
    kd = (k * jnp.exp(bl - b)).astype(BF16)
    el = jnp.exp(bl)
    outs = []
    for i, cs in enumerate(heads):
        st = st_ref[i]
        o = jnp.dot(a[i], v16[:, cs], preferred_element_type=F32)
        o = o + jnp.sum(qk[:, cs], axis=1, keepdims=True) * v[:, cs]
        o = o + lax.dot_general(qb[:, cs], st.astype(BF16), _NT, preferred_element_type=F32)
        st_ref[i] = st * el[:, cs] + jnp.dot(v[:, cs].T.astype(BF16), kd[:, cs],
                                             preferred_element_type=F32)
        outs.append(o)
    return outs


def _hgrn_kernel(q_ref, f_ref, v_ref, g_ref, lb_ref, gh_ref, tri_ref, mask_ref, sign_ref,
                 qd_ref, fd_ref, vd_ref, gd_ref, lbd_ref, sd_ref, *rest, layer, n_chunks, first_call):
    o_ref, s_ref, od_ref, sn_ref, st_ref, kdec, fdec, qfdec, orow = rest[-9:]
    t = pl.program_id(2)
    n_dec = qd_ref.shape[0]
    per = n_dec // n_chunks
    depth = sn_ref.shape[0] if first_call else None

    @pl.when(t == 0)
    def _():
        st_ref[...] = jnp.zeros_like(st_ref)

    lb = _lower_bound(lb_ref, layer)
    gh = gh_ref[layer:layer + 1, :]

    qd, kd, fgd = _gates(qd_ref[...], fd_ref[...], _lower_bound(lbd_ref, layer))
    kdec[...] = kd
    fdec[...] = fgd
    qfdec[...] = qd * fgd
    qk_dec = jnp.sum(qd * kd, axis=1, keepdims=True)
    if first_call:
        for other in range(depth):
            if other != layer:
                sn_ref[other] = jnp.zeros(sn_ref.shape[1:], F32)
    pad = jnp.zeros((HGRN_DK - per, HGRN_DK), F32)

    def body(ci, carry):
        rows = pl.ds(pl.multiple_of(ci * CHUNK, CHUNK), CHUNK)
        outs = _hgrn_chunk(q_ref[rows, :], f_ref[rows, :], v_ref[rows, :], lb, tri_ref, mask_ref,
                           sign_ref, st_ref)
        o = jnp.concatenate([_rms(oh, gh) for oh in outs], axis=1)
        o_ref[rows, :] = (o * _silu(g_ref[rows, :])).astype(o_ref.dtype)

        r0 = pl.multiple_of(ci * per, per)
        grp = pl.ds(r0, per)
        ft = jnp.concatenate([fdec[grp, :], pad], axis=0).T
        kt = jnp.concatenate([kdec[grp, :], pad], axis=0).T
        v_grp = vd_ref[grp, :]
        qf_grp = qfdec[grp, :].astype(BF16)
        o_rows = []
        for r in range(per):
            s_old = sd_ref[r0 + r]
            s_new = ft[:, r:r + 1] * s_old + kt[:, r:r + 1] * v_grp[r:r + 1, :]
            if first_call:
                sn_ref[layer, r0 + r] = s_new
            else:
                sn_ref[r0 + r] = s_new
            o_rows.append(jnp.dot(qf_grp[r:r + 1, :], s_old.astype(BF16), preferred_element_type=F32))
        orow[grp, :] = jnp.concatenate(o_rows, axis=0)
        return carry

    lax.fori_loop(0, n_chunks, body, 0, unroll=2)
    o_dec = orow[...] + qk_dec * vd_ref[...]
    od_ref[...] = (_rms(o_dec, gh) * _silu(gd_ref[...])).astype(od_ref.dtype)

    @pl.when(t == pl.num_programs(2) - 1)
    def _():
        for hh in range(HEADS_PER_STEP):
            s_ref[hh] = st_ref[hh].T


def _hgrn(proj_p, proj_s, state, lb_logits, g_head, consts, layer, batch, seq, heads, new_state=None):
    hp = HEADS_PER_STEP
    n_hp = heads // hp
    wid = hp * HGRN_DK
    nt = seq // SEQ_TILE
    n_chunks = SEQ_TILE // CHUNK
    depth, dec_batch = state.shape[0], state.shape[1]
    n_dec = dec_batch // nt
    assert batch * n_hp == heads and dec_batch % nt == 0 and n_dec // n_chunks == SUBLANES
    first_call = new_state is None
    dhead = lambda b, p: b * n_hp + p
    pcol = lambda grp: (lambda b, p, t: (b * nt + t, grp * n_hp + p))
    dcol = lambda grp: (lambda b, p, t: (t, grp * heads + dhead(b, p)))
    tri, masks, signs = consts
    in_specs = [pl.BlockSpec((SEQ_TILE, wid), pcol(0)),
                pl.BlockSpec((SEQ_TILE, wid), pcol(1)),
                pl.BlockSpec((SEQ_TILE, wid), pcol(2)),
                pl.BlockSpec((SEQ_TILE, wid), pcol(3)),
                pl.BlockSpec((depth, wid), lambda b, p, t: (0, p)),
                pl.BlockSpec((depth, HGRN_DV), lambda b, p, t: (0, 0)),
                pl.BlockSpec(tri.shape, lambda b, p, t: (0, 0)),
                pl.BlockSpec(masks.shape, lambda b, p, t: (0, 0, 0)),
                pl.BlockSpec(signs.shape, lambda b, p, t: (0, 0, 0)),
                pl.BlockSpec((n_dec, HGRN_DK), dcol(0)),
                pl.BlockSpec((n_dec, HGRN_DK), dcol(1)),
                pl.BlockSpec((n_dec, HGRN_DV), dcol(2)),
                pl.BlockSpec((n_dec, HGRN_DV), dcol(3)),
                pl.BlockSpec((depth, HGRN_DK), lambda b, p, t: (0, dhead(b, p))),
                pl.BlockSpec((None, n_dec, None, HGRN_DK, HGRN_DV),
                             lambda b, p, t: (layer, t, dhead(b, p), 0, 0))]
    operands = [proj_p, proj_p, proj_p, proj_p, lb_logits, g_head, tri, masks, signs,
                proj_s, proj_s, proj_s, proj_s, lb_logits, state]
    aliases = {}
    if first_call:
        state_out = pl.BlockSpec((depth, n_dec, None, HGRN_DK, HGRN_DV),
                                 lambda b, p, t: (0, t, dhead(b, p), 0, 0))
    else:
        state_out = pl.BlockSpec((None, n_dec, None, HGRN_DK, HGRN_DV),
                                 lambda b, p, t: (layer, t, dhead(b, p), 0, 0))
        in_specs.append(pl.BlockSpec(memory_space=pl.ANY))
        operands.append(new_state)
        aliases = {len(operands) - 1: 3}
    return pl.pallas_call(
        functools.partial(_hgrn_kernel, layer=layer, n_chunks=n_chunks, first_call=first_call),
        grid=(batch, n_hp, nt),
        in_specs=in_specs,
        out_specs=[pl.BlockSpec((SEQ_TILE, wid), lambda b, p, t: (b * nt + t, p)),
                   pl.BlockSpec((None, hp, HGRN_DK, HGRN_DV), lambda b, p, t: (b, p, 0, 0)),
                   pl.BlockSpec((n_dec, HGRN_DV), lambda b, p, t: (t, dhead(b, p))),
                   state_out],
        out_shape=[jax.ShapeDtypeStruct((batch * seq, heads * HGRN_DV), BF16),
                   jax.ShapeDtypeStruct((batch, heads, HGRN_DK, HGRN_DV), F32),
                   jax.ShapeDtypeStruct((dec_batch, heads * HGRN_DV), BF16),
                   jax.ShapeDtypeStruct(state.shape, F32)],
        scratch_shapes=[pltpu.VMEM((hp, HGRN_DV, HGRN_DK), F32),
                        pltpu.VMEM((n_dec, HGRN_DK), F32), pltpu.VMEM((n_dec, HGRN_DK), F32),
                        pltpu.VMEM((n_dec, HGRN_DK), F32), pltpu.VMEM((n_dec, HGRN_DV), F32)],
        input_output_aliases=aliases,
        compiler_params=_cparams(3),
        name="hgrn",
    )(*operands)


def _pool_sample_kernel(u_ref, buf_ref, wp_ref, ps_ref, o_ref, *, d_pg, n_buf):
    for gi, w in enumerate(POOL_WINDOWS):
        cs = slice(gi * d_pg, (gi + 1) * d_pg)
        u = u_ref[:, cs]
        s = u
        for j in range(1, w):
            s = s + buf_ref[n_buf - j, :, cs]
        pooled = s / float(min(PAST_LEN + 1, w)) - u
        pm = jnp.dot(pooled.astype(BF16), wp_ref[gi].astype(BF16), preferred_element_type=F32)
        o_ref[:, cs] = (pm * ps_ref[:, cs]).astype(o_ref.dtype)


def _pool_sample(proj, buf_t, w_pool, pool_scale, layer, u_col_blk, dec_batch):
    d_pool = pool_scale.shape[1]
    n_pool, d_pg = w_pool.shape[1], w_pool.shape[2]
    n_buf = buf_t.shape[0]
    return pl.pallas_call(
        functools.partial(_pool_sample_kernel, d_pg=d_pg, n_buf=n_buf),
        grid=(1,),
        in_specs=[pl.BlockSpec((dec_batch, d_pool), lambda i: (0, u_col_blk)),
                  pl.BlockSpec(buf_t.shape, lambda i: (0, 0, 0)),
                  pl.BlockSpec((None, n_pool, d_pg, d_pg), lambda i: (layer, 0, 0, 0)),
                  pl.BlockSpec((None, 1, d_pool), lambda i: (layer, 0, 0))],
        out_specs=pl.BlockSpec((dec_batch, d_pool), lambda i: (0, 0)),
        out_shape=jax.ShapeDtypeStruct((dec_batch, d_pool), BF16),
        compiler_params=_cparams(1),
        name="pool_sample",
    )(proj, buf_t, w_pool, pool_scale.reshape(-1, 1, d_pool))


def kernel(x_prompt, x_sample, state_hgrn, state_pool, p_prompt, p_sample, g_mix, w_in, lb_logits,
           g_head, w_pool, pool_scale, w_out, g_ffn, w_gate_up, w_down, w_ple, g_ple, w_ple_gate,
           g_final):
    batch, seq, d_model = x_prompt.shape
    dec_batch = x_sample.shape[0]
    depth = w_in.shape[0]
    heads = lb_logits.shape[1] // HGRN_DK
    d_hgrn = heads * HGRN_DV
    d_pool = pool_scale.shape[1]
    d_ff = w_down.shape[1]
    pool_buf = state_pool.shape[2]
    mp = batch * seq
    assert x_sample.shape[1] == 1 and seq % SEQ_TILE == 0 and heads % HEADS_PER_STEP == 0
    assert dec_batch % ROW_SUBTILE == 0
    assert w_in.shape[2] == 2 * heads * HGRN_DK + 2 * d_hgrn + d_pool

    bm = 1024
    bm_row = 512
    u_col_blk = (2 * heads * HGRN_DK + 2 * d_hgrn) // d_pool
    consts = _chunk_constants()

    h_p = x_prompt.reshape(mp, d_model)
    h_s = x_sample.reshape(dec_batch, d_model)
    pe_p = p_prompt.reshape(depth, mp, -1)
    pe_s = p_sample.reshape(depth, dec_batch, -1)

    s_prompt, pool_prompt, pool_sample = [], [], []
    new_state = None
    n_p, n_s = h_p, h_s
    for l in range(depth):
        proj_p, proj_s = _matmul([n_p], [n_s], w_in, l, n_cols=w_in.shape[2], bm=bm, bn=1024,
                                 norm_gain=g_mix[0].reshape(1, d_model) if l == 0 else None,
                                 name="proj_in")

        o_p, st_p, o_s, new_state = _hgrn(proj_p, proj_s, state_hgrn, lb_logits, g_head, consts, l,
                                          batch, seq, heads, new_state)
        buf_t = jnp.transpose(state_pool[l], (1, 0, 2))
        pm_s = _pool_sample(proj_s, buf_t, w_pool, pool_scale, l, u_col_blk, dec_batch)
        s_prompt.append(st_p)
        u_off = u_col_blk * d_pool
        pool_prompt.append(proj_p.reshape(batch, seq, -1)[:, seq - pool_buf:, u_off:])
        pool_sample.append(jnp.concatenate([state_pool[l][:, 1:], proj_s[:, None, u_off:]], axis=1))

        h_p, n_p, h_s, n_s = _proj_out(o_p, proj_p, h_p, o_s, pm_s, h_s, w_out, w_pool, pool_scale, g_ffn,
                                       l, bm_row // 2, seq, u_col_blk)
        act_p, act_s = _matmul([n_p], [n_s], w_gate_up, l, n_cols=d_ff, bm=bm, bn=512,
                               mode="swiglu", up_offset=d_ff, out_dtypes=(BF16,), name="ffn_up")
        h_p, h_s = _matmul([act_p], [act_s], w_down, l, n_cols=d_model, bm=bm // 2, bn=512,
                           mode="res", extras_p=(h_p,), extras_s=(h_s,), name="ffn_down")
        if l + 1 < depth:
            h_p, n_p, h_s, n_s = _ple(h_p, pe_p, h_s, pe_s, w_ple_gate, w_ple, g_ple,
                                      g_mix[l + 1].reshape(1, d_model), l, bm_row,
                                      write_h=True, norm_dtype=BF16)
        else:
            y_p, y_s = _ple(h_p, pe_p, h_s, pe_s, w_ple_gate, w_ple, g_ple,
                            g_final.reshape(1, d_model), l, bm_row, write_h=False, norm_dtype=F32)

    return (y_p.reshape(batch, seq, d_model), y_s.reshape(dec_batch, 1, d_model),
            jnp.stack(s_prompt), jnp.stack(pool_prompt), new_state, jnp.stack(pool_sample))
```
